```python
import jax
import jax.numpy as jnp
from jax import lax
import numpy as np

D_MODEL = 1024
BATCH = 16
SEQ = 2048
DEPTH = 2

N_BRANCH = 4
BRANCH_W = 512
EPS = 1e-6
ROPE_THETA = 10000.0
CHUNK = 128

CONV_DIM = 512
CONV_W = 3
RET_HEADS = 4
RET_DK = 64
RET_DV = 128
SG_DIM = 512
SG_GROUPS = 4
SG_GROUP_DIM = SG_DIM // SG_GROUPS
MLA_HEADS = 8
MLA_NOPE = 64
MLA_ROPE = 32
MLA_V = 64
MLA_Q_RANK = 384
MLA_KV_RANK = 256
Q_BLOCK = 128
D_FF = ((8 * D_MODEL // 3 + 255) // 256) * 256

SPLIT_SIZES = (CONV_DIM, CONV_DIM, CONV_DIM,
               RET_HEADS * RET_DK, RET_HEADS * RET_DK, RET_HEADS * RET_DV, RET_HEADS * RET_DV,
               SG_DIM, SG_DIM,
               MLA_Q_RANK, MLA_KV_RANK, MLA_ROPE,
               N_BRANCH * D_MODEL)
N_IN = int(sum(SPLIT_SIZES))
SPLIT_POINTS = tuple(int(v) for v in np.cumsum(SPLIT_SIZES)[:-1])

kernel_name = 'hybrid_gated_mixer_block'


def rms_norm(x, g):
    xf = x.astype(jnp.float32)
    y = xf * lax.rsqrt(jnp.mean(xf * xf, axis=-1, keepdims=True) + EPS)
    return (y * g.astype(jnp.float32)).astype(x.dtype)


def ln_plain(x):
    xf = x.astype(jnp.float32)
    mu = jnp.mean(xf, axis=-1, keepdims=True)
    var = jnp.mean(jnp.square(xf - mu), axis=-1, keepdims=True)
    return (xf - mu) * lax.rsqrt(var + EPS)


def layer_norm(x, g, b):
    return (ln_plain(x) * g.astype(jnp.float32) + b.astype(jnp.float32)).astype(x.dtype)


def rope_tables(positions, dim):
    inv = ROPE_THETA ** (-jnp.arange(0, dim, 2, dtype=jnp.float32) / dim)
    ang = positions.astype(jnp.float32)[..., None] * inv
    return jnp.cos(ang), jnp.sin(ang)


def apply_rope(x, cos, sin):
    c = cos[:, :, None, :]
    s = sin[:, :, None, :]
    x1, x2 = jnp.split(x, 2, axis=-1)
    out = jnp.concatenate([x1 * c - x2 * s, x1 * s + x2 * c], axis=-1)
    return out.astype(x.dtype)


def causal_conv(x, w):
    seq = x.shape[1]
    xp = jnp.pad(x, ((0, 0), (CONV_W - 1, 0), (0, 0)))
    return sum(w[i] * xp[:, i:i + seq] for i in range(CONV_W))


def retention(q, k, v, cos, sin):
    bsz, seq = q.shape[0], q.shape[1]
    n_chunks = seq // CHUNK
    q = apply_rope(q, cos, sin)
    k = apply_rope(k, cos, sin) * (RET_DK ** -0.5)
    q = q.reshape(bsz, n_chunks, CHUNK, RET_HEADS, RET_DK)
    k = k.reshape(bsz, n_chunks, CHUNK, RET_HEADS, RET_DK)
    v = v.reshape(bsz, n_chunks, CHUNK, RET_HEADS, RET_DV)
    log_gamma = jnp.log1p(-jnp.exp2(-5.0 - jnp.arange(RET_HEADS, dtype=jnp.float32)))
    idx = jnp.arange(CHUNK, dtype=jnp.float32)
    diff = idx[:, None] - idx[None, :]
    decay = jnp.where(diff >= 0, jnp.exp(jnp.maximum(diff, 0.0)[None] * log_gamma[:, None, None]), 0.0)
    scores = jnp.einsum('bnihd,bnjhd->bnhij', q, k) * decay
    o_inner = jnp.einsum('bnhij,bnjhe->bnihe', scores, v)
    zeta = jnp.exp((CHUNK - 1 - idx)[:, None] * log_gamma[None, :])
    xi = jnp.exp((idx + 1.0)[:, None] * log_gamma[None, :])
    chunk_decay = jnp.exp(CHUNK * log_gamma)
    kv = jnp.einsum('bnjhd,bnjhe,jh->nbhde', k, v, zeta).astype(jnp.float32)

    def step(state, kv_n):
        return chunk_decay[None, :, None, None] * state + kv_n, state

    init = jnp.zeros((bsz, RET_HEADS, RET_DK, RET_DV), jnp.float32)
    _, prev_states = lax.scan(step, init, kv)
    o_cross = jnp.einsum('bnihd,nbhde,ih->bnihe', q, prev_states, xi)
    o = ln_plain(o_inner + o_cross).astype(v.dtype)
    return o.reshape(bsz, seq, RET_HEADS * RET_DV)


def spatial_gating(u, v, ln_g, ln_b, ws, bs):
    bsz, seq = u.shape[0], u.shape[1]
    n_chunks = seq // CHUNK
    v = layer_norm(v, ln_g, ln_b).reshape(bsz, n_chunks, CHUNK, SG_GROUPS, SG_GROUP_DIM)
    w = jnp.tril(ws)
    s = jnp.einsum('gij,bnjgd->bnigd', w, v) + bs.T[None, None, :, :, None]
    return u * s.reshape(bsz, seq, SG_DIM)


def mla(c_q, c_kv, k_pe, q_norm, w_uq, kv_norm, w_ukv, cos, sin):
    bsz, seq = c_q.shape[0], c_q.shape[1]
    q = (rms_norm(c_q, q_norm) @ w_uq).reshape(bsz, seq, MLA_HEADS, MLA_NOPE + MLA_ROPE)
    q_nope, q_pe = q[..., :MLA_NOPE], apply_rope(q[..., MLA_NOPE:], cos, sin)
    kv = (rms_norm(c_kv, kv_norm) @ w_ukv).reshape(bsz, seq, MLA_HEADS, MLA_NOPE + MLA_V)
    k_nope, v = kv[..., :MLA_NOPE], kv[..., MLA_NOPE:]
    k_pe = apply_rope(k_pe[:, :, None, :], cos, sin)[:, :, 0]
    scale = (MLA_NOPE + MLA_ROPE) ** -0.5
    outs = []
    for blk in range(seq // Q_BLOCK):
        q0 = blk * Q_BLOCK
        kend = q0 + Q_BLOCK
        s = (jnp.einsum('bqhd,bkhd->bhqk', q_nope[:, q0:kend], k_nope[:, :kend])
             + jnp.einsum('bqhd,bkd->bhqk', q_pe[:, q0:kend], k_pe[:, :kend]))
        s = s.astype(jnp.float32) * scale
        mask = jnp.arange(kend)[None, :] <= (q0 + jnp.arange(Q_BLOCK))[:, None]
        s = jnp.where(mask, s, jnp.float32(-1e30))
        p = jax.nn.softmax(s, axis=-1).astype(v.dtype)
        outs.append(jnp.einsum('bhqk,bkhd->bqhd', p, v[:, :kend]))
    o = jnp.concatenate(outs, axis=1)
    return o.reshape(bsz, seq, MLA_HEADS * MLA_V)


def hybrid_layer(x, cos_r, sin_r, cos_m, sin_m, norm_mix, w_in, b_gate, conv_w,
                 sg_ln_g, sg_ln_b, sg_ws, sg_b, mla_q_norm, mla_w_uq, mla_kv_norm, mla_w_ukv,
                 w_branch, w_out, norm_ffn, w_ffn_in, w_ffn_out):
    bsz, seq = x.shape[0], x.shape[1]
    h = rms_norm(x, norm_mix)
    proj = h @ w_in
    (a_b, a_c, a_x, r_q, r_k, r_v, r_g, s_u, s_v,
     m_cq, m_ckv, m_kpe, gate_logits) = jnp.split(proj, SPLIT_POINTS, axis=-1)
    y_a = a_b * causal_conv(a_c * a_x, conv_w)
    y_r = jax.nn.silu(r_g) * retention(
        r_q.reshape(bsz, seq, RET_HEADS, RET_DK), r_k.reshape(bsz, seq, RET_HEADS, RET_DK),
        r_v.reshape(bsz, seq, RET_HEADS, RET_DV), cos_r, sin_r)
    y_s = spatial_gating(jax.nn.gelu(s_u), jax.nn.gelu(s_v), sg_ln_g, sg_ln_b, sg_ws, sg_b)
    y_m = mla(m_cq, m_ckv, m_kpe, mla_q_norm, mla_w_uq, mla_kv_norm, mla_w_ukv, cos_m, sin_m)
    gates = jax.nn.sigmoid(gate_logits.reshape(bsz, seq, N_BRANCH, D_MODEL) + b_gate)
    merged = sum(gates[:, :, i] * (y @ w_branch[i]) for i, y in enumerate((y_a, y_r, y_s, y_m)))
    x = x + merged @ w_out
    g, u = jnp.split(rms_norm(x, norm_ffn) @ w_ffn_in, 2, axis=-1)
    return x + (jax.nn.silu(g) * u) @ w_ffn_out


def setup_inputs(seed: int = 0) -> dict:
    key = jax.random.key(seed)
    ks = jax.random.split(key, 24)
    f32 = jnp.float32

    def nrm(k, shape, scale):
        return jax.random.normal(k, shape, f32) * scale

    def gain(k, shape):
        return 1.0 + 0.02 * jax.random.normal(k, shape, f32)

    offsets = jax.random.randint(ks[1], (BATCH, 1), 0, 4096, dtype=jnp.int32)
    positions = offsets + jnp.arange(SEQ, dtype=jnp.int32)[None, :]
    return {
        'x': jax.random.normal(ks[0], (BATCH, SEQ, D_MODEL), f32),
        'positions': positions,
        'norm_mix': gain(ks[2], (DEPTH, D_MODEL)),
        'w_in': nrm(ks[3], (DEPTH, D_MODEL, N_IN), D_MODEL ** -0.5),
        'b_gate': nrm(ks[4], (DEPTH, N_BRANCH, D_MODEL), 0.02),
        'conv_w': nrm(ks[5], (DEPTH, CONV_W, CONV_DIM), CONV_W ** -0.5),
        'sg_ln_g': gain(ks[6], (DEPTH, SG_DIM)),
        'sg_ln_b': nrm(ks[7], (DEPTH, SG_DIM), 0.02),
        'sg_ws': nrm(ks[8], (DEPTH, SG_GROUPS, CHUNK, CHUNK), CHUNK ** -0.5),
        'sg_b': gain(ks[9], (DEPTH, SG_GROUPS, CHUNK)),
        'mla_q_norm': gain(ks[10], (DEPTH, MLA_Q_RANK)),
        'mla_w_uq': nrm(ks[11], (DEPTH, MLA_Q_RANK, MLA_HEADS * (MLA_NOPE + MLA_ROPE)), MLA_Q_RANK ** -0.5),
        'mla_kv_norm': gain(ks[12], (DEPTH, MLA_KV_RANK)),
        'mla_w_ukv': nrm(ks[13], (DEPTH, MLA_KV_RANK, MLA_HEADS * (MLA_NOPE + MLA_V)), MLA_KV_RANK ** -0.5),
        'w_branch': nrm(ks[14], (DEPTH, N_BRANCH, BRANCH_W, D_MODEL), BRANCH_W ** -0.5),
        'w_out': nrm(ks[15], (DEPTH, D_MODEL, D_MODEL), D_MODEL ** -0.5),
        'norm_ffn': gain(ks[16], (DEPTH, D_MODEL)),
        'w_ffn_in': nrm(ks[17], (DEPTH, D_MODEL, 2 * D_FF), D_MODEL ** -0.5),
        'w_ffn_out': nrm(ks[18], (DEPTH, D_FF, D_MODEL), D_FF ** -0.5),
        'final_norm': gain(ks[19], (D_MODEL,)),
    }


def reference(x, positions, norm_mix, w_in, b_gate, conv_w, sg_ln_g, sg_ln_b, sg_ws, sg_b,
              mla_q_norm, mla_w_uq, mla_kv_norm, mla_w_ukv, w_branch, w_out, norm_ffn,
              w_ffn_in, w_ffn_out, final_norm):
    cos_r, sin_r = rope_tables(positions, RET_DK)
    cos_m, sin_m = rope_tables(positions, MLA_ROPE)
    for l in range(DEPTH):
        x = hybrid_layer(x, cos_r, sin_r, cos_m, sin_m, norm_mix[l], w_in[l], b_gate[l], conv_w[l],
                         sg_ln_g[l], sg_ln_b[l], sg_ws[l], sg_b[l], mla_q_norm[l], mla_w_uq[l],
                         mla_kv_norm[l], mla_w_ukv[l], w_branch[l], w_out[l], norm_ffn[l],
                         w_ffn_in[l], w_ffn_out[l])
    return rms_norm(x, final_norm)
```

```python
import functools

import numpy as np
import jax
import jax.numpy as jnp
from jax import lax
from jax.experimental import pallas as pl
from jax.experimental.pallas import tpu as pltpu

F32 = jnp.float32
BF16 = jnp.bfloat16

D_MODEL = 1024
N_BRANCH = 4
BRANCH_W = 512
EPS = 1e-6
ROPE_THETA = 10000.0
CHUNK = 128
CONV_DIM = 512
CONV_W = 3
RET_HEADS = 4
RET_DK = 64
RET_DV = 128
SG_DIM = 512
SG_GROUPS = 4
MLA_HEADS = 8
MLA_NOPE = 64
MLA_ROPE = 32
MLA_V = 64
MLA_Q_RANK = 384
MLA_KV_RANK = 256
D_FF = 2816

LANES = 128
MLA_HEAD_PAD = 128
CONV_HALO = 8
FFN_COLS = 256
VMEM_LIMIT = 56 * 1024 * 1024

TS_FRONT = 256
TQ_ATTN = 256
TM_BACK = 512
TS_TABLE = 512

_OFF = np.cumsum([0, 512, 512, 512, 256, 256, 512, 512, 512, 512, 384, 256, 32, 4096])


def _dot(a, b):
    return jnp.dot(a, b, preferred_element_type=F32)


def _dot_nt(a, b):
    return lax.dot_general(a, b, (((1,), (1,)), ((), ())), preferred_element_type=F32)


def _rms(x, g):
    return x * lax.rsqrt(jnp.mean(x * x, axis=-1, keepdims=True) + EPS) * g


def _sigmoid(x):
    return 1.0 / (1.0 + jnp.exp(-x))


def _const_spec(shape):
    nd = len(shape)
    return pl.BlockSpec(shape, lambda *_: (0,) * nd, pipeline_mode=pl.Buffered(1))


def _rope_table_kernel(pos_ref, inv_ref, tab_ref):
    pos = pos_ref[0]
    ang_r = pos * inv_ref[0:1, :]
    ang_m = pos * inv_ref[1:2, :]
    tab_ref[0, :, 0:128] = jnp.cos(ang_r)
    tab_ref[0, :, 128:256] = jnp.sin(ang_r)
    tab_ref[0, :, 256:384] = jnp.cos(ang_m)
    tab_ref[0, :, 384:512] = jnp.sin(ang_m)


def _rope_tables(positions):
    bsz, seq = positions.shape
    inv_r = ROPE_THETA ** (-jnp.arange(0, RET_DK, 2, dtype=F32) / RET_DK)
    inv_m = ROPE_THETA ** (-jnp.arange(0, MLA_ROPE, 2, dtype=F32) / MLA_ROPE)
    lane_r = jnp.tile(inv_r, LANES // inv_r.shape[0])
    lane_m = jnp.concatenate([jnp.zeros((MLA_NOPE,), F32), inv_m, inv_m,
                              jnp.zeros((MLA_HEAD_PAD - MLA_NOPE - MLA_ROPE,), F32)])
    inv = jnp.zeros((8, LANES), F32).at[0].set(lane_r).at[1].set(lane_m)
    pos_b = jnp.broadcast_to(positions.astype(F32)[:, :, None], (bsz, seq, LANES))
    ts = min(TS_TABLE, seq)
    return pl.pallas_call(
        _rope_table_kernel,
        out_shape=jax.ShapeDtypeStruct((bsz, seq, 4 * LANES), F32),
        grid=(bsz, seq // ts),
        in_specs=[pl.BlockSpec((1, ts, LANES), lambda b, s: (b, s, 0)),
                  pl.BlockSpec((8, LANES), lambda b, s: (0, 0))],
        out_specs=pl.BlockSpec((1, ts, 4 * LANES), lambda b, s: (b, s, 0)),
        compiler_params=pltpu.CompilerParams(dimension_semantics=("arbitrary", "arbitrary")),
        name="rope_tables",
    )(pos_b, inv)


def _mix_front_kernel(x_ref, tab_ref, nrm_ref, wa_ref, wr_ref, ws_ref, wm_ref, wg_ref, bg_ref,
                      cw_ref, lng_ref, lnb_ref, sgw_ref, sgb_ref, qn_ref, wqa_ref, wqb_ref,
                      kvn_ref, wkk_ref, wkv_ref, wb_ref, dec_ref, xi_ref, zeta_ref,
                      mp_ref, gm_ref, q_ref, k_ref, v_ref,
                      conv_sc, state_sc, *, ts, chunk_decay):
    s_idx = pl.program_id(1)

    @pl.when(s_idx == 0)
    def _():
        conv_sc[0:CONV_HALO, :] = jnp.zeros((CONV_HALO, CONV_DIM), F32)
        state_sc[...] = jnp.zeros_like(state_sc)

    x = x_ref[0]
    hb = _rms(x, nrm_ref[...]).astype(BF16)
    tab = tab_ref[0]
    cr, sr = tab[:, 0:128], tab[:, 128:256]
    cm, sm = tab[:, 256:384], tab[:, 384:512]

    pa = _dot(hb, wa_ref[...])
    a_b, a_c, a_x = pa[:, 0:512], pa[:, 512:1024], pa[:, 1024:1536]
    p = a_c * a_x
    conv_sc[CONV_HALO:CONV_HALO + ts, :] = p
    p1 = conv_sc[CONV_HALO - 1:CONV_HALO - 1 + ts, :]
    p2 = conv_sc[CONV_HALO - 2:CONV_HALO - 2 + ts, :]
    y_a = a_b * (cw_ref[0:1, :] * p2 + cw_ref[1:2, :] * p1 + cw_ref[2:3, :] * p)
    conv_sc[0:CONV_HALO, :] = p[ts - CONV_HALO:ts, :]
    g_a = _sigmoid(_dot(hb, wg_ref[:, 0:1024]) + bg_ref[0:1, :])
    mp = g_a * _dot(y_a.astype(BF16), wb_ref[0])

    pr = _dot(hb, wr_ref[...])
    cr2 = jnp.concatenate([cr, cr], axis=1)
    sr2 = jnp.concatenate([sr, sr], axis=1)
    qr = pr[:, 0:256] * cr2 + pr[:, 256:512] * sr2
    kr = (pr[:, 512:768] * cr2 + pr[:, 768:1024] * sr2) * (RET_DK ** -0.5)
    rv = pr[:, 1024:1536].astype(BF16)
    rg = pr[:, 1536:2048]
    y_r_chunks = []
    for c in range(ts // CHUNK):
        rows = slice(c * CHUNK, (c + 1) * CHUNK)
        qc, kc = qr[rows], kr[rows]
        qcb, kcb = qc.astype(BF16), kc.astype(BF16)
        qx = (qc * xi_ref[...]).astype(BF16)
        kz = kc * zeta_ref[...]
        heads = []
        for h in range(RET_HEADS):
            hs = slice(h * RET_DK, (h + 1) * RET_DK)
            vs = slice(h * RET_DV, (h + 1) * RET_DV)
            vh = rv[rows, vs]
            scores = _dot_nt(qcb[:, hs], kcb[:, hs]) * dec_ref[h]
            state = state_sc[h]
            o = _dot(scores.astype(BF16), vh) + _dot(qx[:, hs], state.astype(BF16))
            mu = jnp.mean(o, axis=-1, keepdims=True)
            oc = o - mu
            on = oc * lax.rsqrt(jnp.mean(oc * oc, axis=-1, keepdims=True) + EPS)
            g = rg[rows, vs]
            heads.append(g * _sigmoid(g) * on)
            kzt = jnp.transpose(kz[:, hs]).astype(BF16)
            state_sc[h] = chunk_decay[h] * state + _dot(kzt, vh)
        y_r_chunks.append(jnp.concatenate(heads, axis=1))
    y_r = jnp.concatenate(y_r_chunks, axis=0)
    g_r = _sigmoid(_dot(hb, wg_ref[:, 1024:2048]) + bg_ref[1:2, :])
    mp = mp + g_r * _dot(y_r.astype(BF16), wb_ref[1])

    ps = _dot(hb, ws_ref[...])
    u = jax.nn.gelu(ps[:, 0:512])
    sv = jax.nn.gelu(ps[:, 512:1024])
    mu = jnp.mean(sv, axis=-1, keepdims=True)
    svc = sv - mu
    vln = svc * lax.rsqrt(jnp.mean(svc * svc, axis=-1, keepdims=True) + EPS) * lng_ref[...] + lnb_ref[...]
    vlnb = vln.astype(BF16)
    row = lax.broadcasted_iota(jnp.int32, (CHUNK, CHUNK), 0)
    col = lax.broadcasted_iota(jnp.int32, (CHUNK, CHUNK), 1)
    wtril = [jnp.where(row >= col, sgw_ref[g], 0.0).astype(BF16) for g in range(SG_GROUPS)]
    s_chunks = []
    for c in range(ts // CHUNK):
        rows = slice(c * CHUNK, (c + 1) * CHUNK)
        s_chunks.append(jnp.concatenate(
            [_dot(wtril[g], vlnb[rows, g * CHUNK:(g + 1) * CHUNK]) + sgb_ref[g]
             for g in range(SG_GROUPS)], axis=1))
    y_s = u * jnp.concatenate(s_chunks, axis=0)
    g_s = _sigmoid(_dot(hb, wg_ref[:, 2048:3072]) + bg_ref[2:3, :])
    mp = mp + g_s * _dot(y_s.astype(BF16), wb_ref[2])
    mp_ref[0] = mp.astype(BF16)
    gm_ref[0] = _sigmoid(_dot(hb, wg_ref[:, 3072:4096]) + bg_ref[3:4, :]).astype(BF16)

    pm = _dot(hb, wm_ref[...])
    cqn = _rms(pm[:, 0:384], qn_ref[...]).astype(BF16)
    ckvn = _rms(pm[:, 384:640], kvn_ref[...]).astype(BF16)
    cm8 = jnp.concatenate([cm] * MLA_HEADS, axis=1)
    sm8 = jnp.concatenate([sm] * MLA_HEADS, axis=1)
    qf = (_dot(cqn, wqa_ref[...]) * cm8 + _dot(cqn, wqb_ref[...]) * sm8) * ((MLA_NOPE + MLA_ROPE) ** -0.5)
    q_ref[0] = qf.astype(BF16)
    kpe = pm[:, 640:768] * cm + pm[:, 768:896] * sm
    kf = _dot(ckvn, wkk_ref[...]) + jnp.concatenate([kpe] * MLA_HEADS, axis=1)
    k_ref[0] = kf.astype(BF16)
    v_ref[0] = _dot(ckvn, wkv_ref[...]).astype(BF16)


def _rot_cols(w, heads, dh):
    k = w.shape[0]
    w3 = w.reshape(k, heads, dh)
    h2 = dh // 2
    return jnp.concatenate([-w3[..., h2:], w3[..., :h2]], axis=-1).reshape(k, heads * dh)


def _retention_tables():
    log_gamma = jnp.log1p(-jnp.exp2(-5.0 - jnp.arange(RET_HEADS, dtype=F32)))
    idx = jnp.arange(CHUNK, dtype=F32)
    diff = idx[:, None] - idx[None, :]
    decay = jnp.where(diff >= 0, jnp.exp(jnp.maximum(diff, 0.0)[None] * log_gamma[:, None, None]), 0.0)
    zeta = jnp.exp((CHUNK - 1 - idx)[:, None] * log_gamma[None, :])
    xi = jnp.exp((idx + 1.0)[:, None] * log_gamma[None, :])
    zeta_full = jnp.repeat(zeta, RET_DK, axis=1)
    xi_full = jnp.repeat(xi, RET_DK, axis=1)
    lg = np.log1p(-np.exp2(-5.0 - np.arange(RET_HEADS, dtype=np.float64)))
    chunk_decay = tuple(float(v) for v in np.exp(CHUNK * lg))
    return decay, xi_full, zeta_full, chunk_decay


def _prep_front_weights(w_in, mla_w_uq, mla_w_ukv):
    o = _OFF
    wa = w_in[:, o[0]:o[3]]
    r_q, r_k = w_in[:, o[3]:o[4]], w_in[:, o[4]:o[5]]
    wr = jnp.concatenate([r_q, _rot_cols(r_q, RET_HEADS, RET_DK), r_k, _rot_cols(r_k, RET_HEADS, RET_DK),
                          w_in[:, o[5]:o[7]]], axis=1)
    ws = w_in[:, o[7]:o[9]]
    kpe = w_in[:, o[11]:o[12]]
    pad_l = jnp.zeros((D_MODEL, MLA_NOPE), F32)
    pad_r = jnp.zeros((D_MODEL, MLA_HEAD_PAD - MLA_NOPE - MLA_ROPE), F32)
    wm = jnp.concatenate([w_in[:, o[9]:o[11]], pad_l, kpe, pad_r,
                          pad_l, _rot_cols(kpe, 1, MLA_ROPE), pad_r], axis=1)
    wg = w_in[:, o[12]:o[13]]
    dq = MLA_NOPE + MLA_ROPE
    uq = mla_w_uq.reshape(MLA_Q_RANK, MLA_HEADS, dq)
    q_pe = uq[..., MLA_NOPE:]
    zq = jnp.zeros((MLA_Q_RANK, MLA_HEADS, MLA_HEAD_PAD - dq), F32)
    wqa = jnp.concatenate([uq, zq], axis=-1).reshape(MLA_Q_RANK, MLA_HEADS * MLA_HEAD_PAD)
    q_rot = jnp.concatenate([-q_pe[..., MLA_ROPE // 2:], q_pe[..., :MLA_ROPE // 2]], axis=-1)
    wqb = jnp.concatenate([jnp.zeros((MLA_Q_RANK, MLA_HEADS, MLA_NOPE), F32), q_rot, zq],
                          axis=-1).reshape(MLA_Q_RANK, MLA_HEADS * MLA_HEAD_PAD)
    ukv = mla_w_ukv.reshape(MLA_KV_RANK, MLA_HEADS, MLA_NOPE + MLA_V)
    zk = jnp.zeros((MLA_KV_RANK, MLA_HEADS, MLA_HEAD_PAD - MLA_NOPE), F32)
    wkk = jnp.concatenate([ukv[..., :MLA_NOPE], zk], axis=-1).reshape(MLA_KV_RANK, MLA_HEADS * MLA_HEAD_PAD)
    wkv = ukv[..., MLA_NOPE:].reshape(MLA_KV_RANK, MLA_HEADS * MLA_V)
    return [w.astype(BF16) for w in (wa, wr, ws, wm, wg, wqa, wqb, wkk, wkv)]


def _mix_front(x, tab, norm_mix, w_in, b_gate, conv_w, sg_ln_g, sg_ln_b, sg_ws, sg_b,
               mla_q_norm, mla_w_uq, mla_kv_norm, mla_w_ukv, w_branch):
    bsz, seq, _ = x.shape
    ts = min(TS_FRONT, seq)
    wa, wr, ws, wm, wg, wqa, wqb, wkk, wkv = _prep_front_weights(w_in, mla_w_uq, mla_w_ukv)
    wb = w_branch[0:3].astype(BF16)
    decay, xi_full, zeta_full, chunk_decay = _retention_tables()
    sgb = jnp.broadcast_to(sg_b[:, :, None], (SG_GROUPS, CHUNK, CHUNK))
    consts = [norm_mix.reshape(1, D_MODEL), wa, wr, ws, wm, wg, b_gate, conv_w,
              sg_ln_g.reshape(1, SG_DIM), sg_ln_b.reshape(1, SG_DIM), sg_ws, sgb,
              mla_q_norm.reshape(1, MLA_Q_RANK), wqa, wqb, mla_kv_norm.reshape(1, MLA_KV_RANK),
              wkk, wkv, wb, decay, xi_full, zeta_full]
    tile = lambda w: pl.BlockSpec((1, ts, w), lambda b, s: (b, s, 0))
    qk_w = MLA_HEADS * MLA_HEAD_PAD
    out_shape = [jax.ShapeDtypeStruct((bsz, seq, w), BF16)
                 for w in (D_MODEL, D_MODEL, qk_w, qk_w, MLA_HEADS * MLA_V)]
    return pl.pallas_call(
        functools.partial(_mix_front_kernel, ts=ts, chunk_decay=chunk_decay),
        out_shape=out_shape,
        grid=(bsz, seq // ts),
        in_specs=[tile(D_MODEL), tile(4 * LANES)] + [_const_spec(c.shape) for c in consts],
        out_specs=[tile(s.shape[-1]) for s in out_shape],
        scratch_shapes=[pltpu.VMEM((CONV_HALO + ts, CONV_DIM), F32),
                        pltpu.VMEM((RET_HEADS, RET_DK, RET_DV), F32)],
        compiler_params=pltpu.CompilerParams(dimension_semantics=("arbitrary", "arbitrary"),
                                             vmem_limit_bytes=VMEM_LIMIT),
        name="mix_front",
    )(x, tab, *consts)


def _attn_kernel(q_ref, k_ref, v_ref, o_ref, m_sc, l_sc, acc_sc, *, tq):
    qi = pl.program_id(1)
    m_sc[...] = jnp.full(m_sc.shape, -1e30, F32)
    l_sc[...] = jnp.zeros_like(l_sc)
    acc_sc[...] = jnp.zeros_like(acc_sc)

    def block(j, masked):
        ks = pl.multiple_of(j * tq, tq)
        if masked:
            row = lax.broadcasted_iota(jnp.int32, (tq, tq), 0)
            col = lax.broadcasted_iota(jnp.int32, (tq, tq), 1)
            keep = col <= row
        for h in range(MLA_HEADS):
            hp = slice(h * MLA_HEAD_PAD, (h + 1) * MLA_HEAD_PAD)
            s = _dot_nt(q_ref[0, :, hp], k_ref[0, pl.ds(ks, tq), hp])
            if masked:
                s = jnp.where(keep, s, -1e30)
            m_prev = m_sc[h]
            m_new = jnp.maximum(m_prev, jnp.max(s, axis=-1, keepdims=True))
            alpha = jnp.exp(m_prev - m_new)
            p = jnp.exp(s - m_new)
            l_sc[h] = alpha * l_sc[h] + jnp.sum(p, axis=-1, keepdims=True)
            vh = v_ref[0, pl.ds(ks, tq), h * MLA_V:(h + 1) * MLA_V]
            acc_sc[h] = alpha * acc_sc[h] + _dot(p.astype(BF16), vh)
            m_sc[h] = m_new

    def body(j, carry):
        block(j, False)
        return carry

    lax.fori_loop(0, qi, body, 0)
    block(qi, True)
    for h in range(MLA_HEADS):
        o_ref[0, :, h * MLA_V:(h + 1) * MLA_V] = (acc_sc[h] / l_sc[h]).astype(BF16)


def _attention(q, k, v):
    bsz, seq, qk_w = q.shape
    tq = min(TQ_ATTN, seq)
    v_w = v.shape[-1]
    return pl.pallas_call(
        functools.partial(_attn_kernel, tq=tq),
        out_shape=jax.ShapeDtypeStruct((bsz, seq, v_w), BF16),
        grid=(bsz, seq // tq),
        in_specs=[pl.BlockSpec((1, tq, qk_w), lambda b, i: (b, i, 0)),
                  pl.BlockSpec((1, seq, qk_w), lambda b, i: (b, 0, 0)),
                  pl.BlockSpec((1, seq, v_w), lambda b, i: (b, 0, 0))],
        out_specs=pl.BlockSpec((1, tq, v_w), lambda b, i: (b, i, 0)),
        scratch_shapes=[pltpu.VMEM((MLA_HEADS, tq, 1), F32),
                        pltpu.VMEM((MLA_HEADS, tq, 1), F32),
                        pltpu.VMEM((MLA_HEADS, tq, MLA_V), F32)],
        compiler_params=pltpu.CompilerParams(dimension_semantics=("arbitrary", "arbitrary"),
                                             vmem_limit_bytes=VMEM_LIMIT),
        name="mla_attention",
    )(q, k, v)


def _mix_back_kernel(x_ref, mp_ref, gm_ref, ym_ref, wbm_ref, wo_ref, nf_ref, wfi_ref, wfo_ref,
                     fn_ref, o_ref, *, final):
    merged = mp_ref[...].astype(F32) + gm_ref[...].astype(F32) * _dot(ym_ref[...], wbm_ref[...])
    x1 = x_ref[...] + _dot(merged.astype(BF16), wo_ref[...])
    hn = _rms(x1, nf_ref[...]).astype(BF16)
    acc = jnp.zeros_like(x1)
    for c in range(D_FF // FFN_COLS):
        g = _dot(hn, wfi_ref[:, c * FFN_COLS:(c + 1) * FFN_COLS])
        u = _dot(hn, wfi_ref[:, D_FF + c * FFN_COLS:D_FF + (c + 1) * FFN_COLS])
        act = (g * _sigmoid(g) * u).astype(BF16)
        acc = acc + _dot(act, wfo_ref[c * FFN_COLS:(c + 1) * FFN_COLS, :])
    x2 = x1 + acc
    if final:
        x2 = _rms(x2, fn_ref[...])
    o_ref[...] = x2


def _mix_back(x, mp, gm, ym, w_branch_m, w_out, norm_ffn, w_ffn_in, w_ffn_out, final_norm, final):
    t = x.shape[0]
    tm = min(TM_BACK, t)
    consts = [w_branch_m.astype(BF16), w_out.astype(BF16), norm_ffn.reshape(1, D_MODEL),
              w_ffn_in.astype(BF16), w_ffn_out.astype(BF16), final_norm.reshape(1, D_MODEL)]
    tile = lambda w: pl.BlockSpec((tm, w), lambda i: (i, 0))
    return pl.pallas_call(
        functools.partial(_mix_back_kernel, final=final),
        out_shape=jax.ShapeDtypeStruct((t, D_MODEL), F32),
        grid=(t // tm,),
        in_specs=[tile(D_MODEL), tile(D_MODEL), tile(D_MODEL), tile(BRANCH_W)]
                 + [_const_spec(c.shape) for c in consts],
        out_specs=tile(D_MODEL),
        compiler_params=pltpu.CompilerParams(dimension_semantics=("arbitrary",),
                                             vmem_limit_bytes=VMEM_LIMIT),
        name="mix_back_ffn",
    )(x, mp, gm, ym, *consts)


def kernel(x, positions, norm_mix, w_in, b_gate, conv_w, sg_ln_g, sg_ln_b, sg_ws, sg_b, mla_q_norm,
           mla_w_uq, mla_kv_norm, mla_w_ukv, w_branch, w_out, norm_ffn, w_ffn_in, w_ffn_out, final_norm):
    bsz, seq, d = x.shape
    depth = w_in.shape[0]
    tab = _rope_tables(positions)
    for l in range(depth):
        mp, gm, q, k, v = _mix_front(x, tab, norm_mix[l], w_in[l], b_gate[l], conv_w[l], sg_ln_g[l],
                                     sg_ln_b[l], sg_ws[l], sg_b[l], mla_q_norm[l], mla_w_uq[l],
                                     mla_kv_norm[l], mla_w_ukv[l], w_branch[l])
        ym = _attention(q, k, v)
        flat = lambda a: a.reshape(bsz * seq, a.shape[-1])
        x = _mix_back(flat(x), flat(mp), flat(gm), flat(ym), w_branch[l, 3], w_out[l], norm_ffn[l],
                      w_ffn_in[l], w_ffn_out[l], final_norm, final=(l == depth - 1)).reshape(bsz, seq, d)
    return x
```

```python
import functools

import numpy as np
import jax
import jax.numpy as jnp
from jax import lax
from jax.experimental import pallas as pl
from jax.experimental.pallas import tpu as pltpu

F32 = jnp.float32
BF16 = jnp.bfloat16

D_MODEL = 1024
N_BRANCH = 4
BRANCH_W = 512
EPS = 1e-6
ROPE_THETA = 10000.0
CHUNK = 128
CONV_DIM = 512
CONV_W = 3
RET_HEADS = 4
RET_DK = 64
RET_DV = 128
SG_DIM = 512
SG_GROUPS = 4
MLA_HEADS = 8
MLA_NOPE = 64
MLA_ROPE = 32
MLA_V = 64
MLA_Q_RANK = 384
MLA_KV_RANK = 256
D_FF = 2816

LANES = 128
MLA_HEAD_PAD = 128
CONV_HALO = 8
FFN_COLS = 256
VMEM_LIMIT = 56 * 1024 * 1024
QK_SCALE_LOG2E = float((MLA_NOPE + MLA_ROPE) ** -0.5 * np.log2(np.e))

TS_FRONT = 256
TQ_ATTN = 256
TM_BACK = 512
TS_TABLE = 512

_OFF = np.cumsum([0, 512, 512, 512, 256, 256, 512, 512, 512, 512, 384, 256, 32, 4096])


def _dot(a, b):
    return jnp.dot(a, b, preferred_element_type=F32)


def _dot_nt(a, b):
    return lax.dot_general(a, b, (((1,), (1,)), ((), ())), preferred_element_type=F32)


def _rms(x, g):
    return x * lax.rsqrt(jnp.mean(x * x, axis=-1, keepdims=True) + EPS) * g


def _sigmoid(x):
    return 1.0 / (1.0 + jnp.exp(-x))


def _const_spec(shape):
    nd = len(shape)
    return pl.BlockSpec(shape, lambda *_: (0,) * nd, pipeline_mode=pl.Buffered(1))


def _rope_table_kernel(pos_ref, inv_ref, tab_ref):
    pos = pos_ref[0]
    ang_r = pos * inv_ref[0:1, :]
    ang_m = pos * inv_ref[1:2, :]
    tab_ref[0, :, 0:128] = jnp.cos(ang_r)
    tab_ref[0, :, 128:256] = jnp.sin(ang_r)
    tab_ref[0, :, 256:384] = jnp.cos(ang_m)
    tab_ref[0, :, 384:512] = jnp.sin(ang_m)


def _rope_tables(positions):
    bsz, seq = positions.shape
    inv_r = ROPE_THETA ** (-jnp.arange(0, RET_DK, 2, dtype=F32) / RET_DK)
    inv_m = ROPE_THETA ** (-jnp.arange(0, MLA_ROPE, 2, dtype=F32) / MLA_ROPE)
    lane_r = jnp.tile(inv_r, LANES // inv_r.shape[0])
    lane_m = jnp.concatenate([jnp.zeros((MLA_NOPE,), F32), inv_m, inv_m,
                              jnp.zeros((MLA_HEAD_PAD - MLA_NOPE - MLA_ROPE,), F32)])
    inv = jnp.zeros((8, LANES), F32).at[0].set(lane_r).at[1].set(lane_m)
    pos_b = jnp.broadcast_to(positions.astype(F32)[:, :, None], (bsz, seq, LANES))
    ts = min(TS_TABLE, seq)
    return pl.pallas_call(
        _rope_table_kernel,
        out_shape=jax.ShapeDtypeStruct((bsz, seq, 4 * LANES), F32),
        grid=(bsz, seq // ts),
        in_specs=[pl.BlockSpec((1, ts, LANES), lambda b, s: (b, s, 0)),
                  pl.BlockSpec((8, LANES), lambda b, s: (0, 0))],
        out_specs=pl.BlockSpec((1, ts, 4 * LANES), lambda b, s: (b, s, 0)),
        compiler_params=pltpu.CompilerParams(dimension_semantics=("arbitrary", "arbitrary")),
        name="rope_tables",
    )(pos_b, inv)


def _mix_front_kernel(x_ref, tab_ref, nrm_ref, wa_ref, wr_ref, ws_ref, wm_ref, wg_ref, bg_ref,
                      cw_ref, lng_ref, lnb_ref, sgw_ref, sgb_ref, qn_ref, wqa_ref, wqb_ref,
                      kvn_ref, wkk_ref, wkv_ref, wb_ref, dec_ref, xi_ref, zeta_ref,
                      mp_ref, gm_ref, q_ref, k_ref, v_ref,
                      conv_sc, state_sc, *, ts, chunk_decay):
    s_idx = pl.program_id(1)

    @pl.when(s_idx == 0)
    def _():
        conv_sc[0:CONV_HALO, :] = jnp.zeros((CONV_HALO, CONV_DIM), F32)
        state_sc[...] = jnp.zeros_like(state_sc)

    x = x_ref[0]
    hb = _rms(x, nrm_ref[...]).astype(BF16)
    tab = tab_ref[0]
    cr, sr = tab[:, 0:128], tab[:, 128:256]
    cm, sm = tab[:, 256:384], tab[:, 384:512]

    pa = _dot(hb, wa_ref[...])
    a_b, a_c, a_x = pa[:, 0:512], pa[:, 512:1024], pa[:, 1024:1536]
    p = a_c * a_x
    conv_sc[CONV_HALO:CONV_HALO + ts, :] = p
    p1 = conv_sc[CONV_HALO - 1:CONV_HALO - 1 + ts, :]
    p2 = conv_sc[CONV_HALO - 2:CONV_HALO - 2 + ts, :]
    y_a = a_b * (cw_ref[0:1, :] * p2 + cw_ref[1:2, :] * p1 + cw_ref[2:3, :] * p)
    conv_sc[0:CONV_HALO, :] = p[ts - CONV_HALO:ts, :]
    g_a = _sigmoid(_dot(hb, wg_ref[:, 0:1024]) + bg_ref[0:1, :])
    mp = g_a * _dot(y_a.astype(BF16), wb_ref[0])

    pr = _dot(hb, wr_ref[...])
    cr2 = jnp.concatenate([cr, cr], axis=1)
    sr2 = jnp.concatenate([sr, sr], axis=1)
    qr = pr[:, 0:256] * cr2 + pr[:, 256:512] * sr2
    kr = (pr[:, 512:768] * cr2 + pr[:, 768:1024] * sr2) * (RET_DK ** -0.5)
    rv = pr[:, 1024:1536].astype(BF16)
    rg = pr[:, 1536:2048]
    y_r_chunks = []
    for c in range(ts // CHUNK):
        rows = slice(c * CHUNK, (c + 1) * CHUNK)
        qc, kc = qr[rows], kr[rows]
        qcb, kcb = qc.astype(BF16), kc.astype(BF16)
        qx = (qc * xi_ref[...]).astype(BF16)
        kz = kc * zeta_ref[...]
        heads = []
        for h in range(RET_HEADS):
            hs = slice(h * RET_DK, (h + 1) * RET_DK)
            vs = slice(h * RET_DV, (h + 1) * RET_DV)
            vh = rv[rows, vs]
            scores = _dot_nt(qcb[:, hs], kcb[:, hs]) * dec_ref[h]
            state = state_sc[h]
            o = _dot(scores.astype(BF16), vh) + _dot(qx[:, hs], state.astype(BF16))
            mu = jnp.mean(o, axis=-1, keepdims=True)
            oc = o - mu
            on = oc * lax.rsqrt(jnp.mean(oc * oc, axis=-1, keepdims=True) + EPS)
            g = rg[rows, vs]
            heads.append(g * _sigmoid(g) * on)
            kzt = jnp.transpose(kz[:, hs]).astype(BF16)
            state_sc[h] = chunk_decay[h] * state + _dot(kzt, vh)
        y_r_chunks.append(jnp.concatenate(heads, axis=1))
    y_r = jnp.concatenate(y_r_chunks, axis=0)
    g_r = _sigmoid(_dot(hb, wg_ref[:, 1024:2048]) + bg_ref[1:2, :])
    mp = mp + g_r * _dot(y_r.astype(BF16), wb_ref[1])

    ps = _dot(hb, ws_ref[...])
    u = jax.nn.gelu(ps[:, 0:512])
    sv = jax.nn.gelu(ps[:, 512:1024])
    mu = jnp.mean(sv, axis=-1, keepdims=True)
    svc = sv - mu
    vln = svc * lax.rsqrt(jnp.mean(svc * svc, axis=-1, keepdims=True) + EPS) * lng_ref[...] + lnb_ref[...]
    vlnb = vln.astype(BF16)
    row = lax.broadcasted_iota(jnp.int32, (CHUNK, CHUNK), 0)
    col = lax.broadcasted_iota(jnp.int32, (CHUNK, CHUNK), 1)
    wtril = [jnp.where(row >= col, sgw_ref[g], 0.0).astype(BF16) for g in range(SG_GROUPS)]
    s_chunks = []
    for c in range(ts // CHUNK):
        rows = slice(c * CHUNK, (c + 1) * CHUNK)
        s_chunks.append(jnp.concatenate(
            [_dot(wtril[g], vlnb[rows, g * CHUNK:(g + 1) * CHUNK]) + sgb_ref[g]
             for g in range(SG_GROUPS)], axis=1))
    y_s = u * jnp.concatenate(s_chunks, axis=0)
    g_s = _sigmoid(_dot(hb, wg_ref[:, 2048:3072]) + bg_ref[2:3, :])
    mp = mp + g_s * _dot(y_s.astype(BF16), wb_ref[2])
    mp_ref[0] = mp.astype(BF16)
    gm_ref[0] = _sigmoid(_dot(hb, wg_ref[:, 3072:4096]) + bg_ref[3:4, :]).astype(BF16)

    pm = _dot(hb, wm_ref[...])
    cqn = _rms(pm[:, 0:384], qn_ref[...]).astype(BF16)
    ckvn = _rms(pm[:, 384:640], kvn_ref[...]).astype(BF16)
    cm8 = jnp.concatenate([cm] * MLA_HEADS, axis=1)
    sm8 = jnp.concatenate([sm] * MLA_HEADS, axis=1)
    qf = (_dot(cqn, wqa_ref[...]) * cm8 + _dot(cqn, wqb_ref[...]) * sm8) * QK_SCALE_LOG2E
    q_ref[0] = qf.astype(BF16)
    kpe = pm[:, 640:768] * cm + pm[:, 768:896] * sm
    kf = _dot(ckvn, wkk_ref[...]) + jnp.concatenate([kpe] * MLA_HEADS, axis=1)
    k_ref[0] = kf.astype(BF16)
    v_ref[0] = _dot(ckvn, wkv_ref[...]).astype(BF16)


def _rot_cols(w, heads, dh):
    k = w.shape[0]
    w3 = w.reshape(k, heads, dh)
    h2 = dh // 2
    return jnp.concatenate([-w3[..., h2:], w3[..., :h2]], axis=-1).reshape(k, heads * dh)


def _retention_tables():
    log_gamma = jnp.log1p(-jnp.exp2(-5.0 - jnp.arange(RET_HEADS, dtype=F32)))
    idx = jnp.arange(CHUNK, dtype=F32)
    diff = idx[:, None] - idx[None, :]
    decay = jnp.where(diff >= 0, jnp.exp(jnp.maximum(diff, 0.0)[None] * log_gamma[:, None, None]), 0.0)
    zeta = jnp.exp((CHUNK - 1 - idx)[:, None] * log_gamma[None, :])
    xi = jnp.exp((idx + 1.0)[:, None] * log_gamma[None, :])
    zeta_full = jnp.repeat(zeta, RET_DK, axis=1)
    xi_full = jnp.repeat(xi, RET_DK, axis=1)
    lg = np.log1p(-np.exp2(-5.0 - np.arange(RET_HEADS, dtype=np.float64)))
    chunk_decay = tuple(float(v) for v in np.exp(CHUNK * lg))
    return decay, xi_full, zeta_full, chunk_decay


def _prep_front_weights(w_in, mla_w_uq, mla_w_ukv):
    o = _OFF
    wa = w_in[:, o[0]:o[3]]
    r_q, r_k = w_in[:, o[3]:o[4]], w_in[:, o[4]:o[5]]
    wr = jnp.concatenate([r_q, _rot_cols(r_q, RET_HEADS, RET_DK), r_k, _rot_cols(r_k, RET_HEADS, RET_DK),
                          w_in[:, o[5]:o[7]]], axis=1)
    ws = w_in[:, o[7]:o[9]]
    kpe = w_in[:, o[11]:o[12]]
    pad_l = jnp.zeros((D_MODEL, MLA_NOPE), F32)
    pad_r = jnp.zeros((D_MODEL, MLA_HEAD_PAD - MLA_NOPE - MLA_ROPE), F32)
    wm = jnp.concatenate([w_in[:, o[9]:o[11]], pad_l, kpe, pad_r,
                          pad_l, _rot_cols(kpe, 1, MLA_ROPE), pad_r], axis=1)
    wg = w_in[:, o[12]:o[13]]
    dq = MLA_NOPE + MLA_ROPE
    uq = mla_w_uq.reshape(MLA_Q_RANK, MLA_HEADS, dq)
    q_pe = uq[..., MLA_NOPE:]
    zq = jnp.zeros((MLA_Q_RANK, MLA_HEADS, MLA_HEAD_PAD - dq), F32)
    wqa = jnp.concatenate([uq, zq], axis=-1).reshape(MLA_Q_RANK, MLA_HEADS * MLA_HEAD_PAD)
    q_rot = jnp.concatenate([-q_pe[..., MLA_ROPE // 2:], q_pe[..., :MLA_ROPE // 2]], axis=-1)
    wqb = jnp.concatenate([jnp.zeros((MLA_Q_RANK, MLA_HEADS, MLA_NOPE), F32), q_rot, zq],
                          axis=-1).reshape(MLA_Q_RANK, MLA_HEADS * MLA_HEAD_PAD)
    ukv = mla_w_ukv.reshape(MLA_KV_RANK, MLA_HEADS, MLA_NOPE + MLA_V)
    zk = jnp.zeros((MLA_KV_RANK, MLA_HEADS, MLA_HEAD_PAD - MLA_NOPE), F32)
    wkk = jnp.concatenate([ukv[..., :MLA_NOPE], zk], axis=-1).reshape(MLA_KV_RANK, MLA_HEADS * MLA_HEAD_PAD)
    wkv = ukv[..., MLA_NOPE:].reshape(MLA_KV_RANK, MLA_HEADS * MLA_V)
    return [w.astype(BF16) for w in (wa, wr, ws, wm, wg, wqa, wqb, wkk, wkv)]


def _mix_front(x, tab, norm_mix, w_in, b_gate, conv_w, sg_ln_g, sg_ln_b, sg_ws, sg_b,
               mla_q_norm, mla_w_uq, mla_kv_norm, mla_w_ukv, w_branch):
    bsz, seq, _ = x.shape
    ts = min(TS_FRONT, seq)
    wa, wr, ws, wm, wg, wqa, wqb, wkk, wkv = _prep_front_weights(w_in, mla_w_uq, mla_w_ukv)
    wb = w_branch[0:3].astype(BF16)
    decay, xi_full, zeta_full, chunk_decay = _retention_tables()
    sgb = jnp.broadcast_to(sg_b[:, :, None], (SG_GROUPS, CHUNK, CHUNK))
    consts = [norm_mix.reshape(1, D_MODEL), wa, wr, ws, wm, wg, b_gate, conv_w,
              sg_ln_g.reshape(1, SG_DIM), sg_ln_b.reshape(1, SG_DIM), sg_ws, sgb,
              mla_q_norm.reshape(1, MLA_Q_RANK), wqa, wqb, mla_kv_norm.reshape(1, MLA_KV_RANK),
              wkk, wkv, wb, decay, xi_full, zeta_full]
    tile = lambda w: pl.BlockSpec((1, ts, w), lambda b, s: (b, s, 0))
    qk_w = MLA_HEADS * MLA_HEAD_PAD
    out_shape = [jax.ShapeDtypeStruct((bsz, seq, w), BF16)
                 for w in (D_MODEL, D_MODEL, qk_w, qk_w, MLA_HEADS * MLA_V)]
    return pl.pallas_call(
        functools.partial(_mix_front_kernel, ts=ts, chunk_decay=chunk_decay),
        out_shape=out_shape,
        grid=(bsz, seq // ts),
        in_specs=[tile(D_MODEL), tile(4 * LANES)] + [_const_spec(c.shape) for c in consts],
        out_specs=[tile(s.shape[-1]) for s in out_shape],
        scratch_shapes=[pltpu.VMEM((CONV_HALO + ts, CONV_DIM), F32),
                        pltpu.VMEM((RET_HEADS, RET_DK, RET_DV), F32)],
        compiler_params=pltpu.CompilerParams(dimension_semantics=("arbitrary", "arbitrary"),
                                             vmem_limit_bytes=VMEM_LIMIT),
        name="mix_front",
    )(x, tab, *consts)


def _attn_kernel(q_ref, k_ref, v_ref, o_ref, s_sc, mx_sc, l_sc, acc_sc, *, tq):
    qi = pl.program_id(1)
    half = tq // 2
    mx_sc[...] = jnp.full(mx_sc.shape, -1e30, F32)
    l_sc[...] = jnp.zeros_like(l_sc)
    acc_sc[...] = jnp.zeros_like(acc_sc)

    def scores(j, masked):
        ks = pl.multiple_of(j * tq, tq)
        if masked:
            row = lax.broadcasted_iota(jnp.int32, (tq, tq), 0)
            col = lax.broadcasted_iota(jnp.int32, (tq, tq), 1)
            keep = col <= row
        for h in range(MLA_HEADS):
            hp = slice(h * MLA_HEAD_PAD, (h + 1) * MLA_HEAD_PAD)
            s = _dot_nt(q_ref[0, :, hp], k_ref[0, pl.ds(ks, tq), hp])
            if masked:
                s = jnp.where(keep, s, -1e30)
            s_sc[j, h] = s
            mx_sc[h] = jnp.maximum(mx_sc[h], jnp.maximum(s[:, :half], s[:, half:]))

    def scores_body(j, carry):
        scores(j, False)
        return carry

    lax.fori_loop(0, qi, scores_body, 0)
    scores(qi, True)

    for h in range(MLA_HEADS):
        mx_sc[h] = jnp.broadcast_to(jnp.max(mx_sc[h], axis=-1, keepdims=True), (tq, half))

    def pv_body(j, carry):
        ks = pl.multiple_of(j * tq, tq)
        for h in range(MLA_HEADS):
            m = mx_sc[h]
            p = jnp.exp2(s_sc[j, h] - jnp.concatenate([m, m], axis=1))
            l_sc[h] += p[:, :half] + p[:, half:]
            vh = v_ref[0, pl.ds(ks, tq), h * MLA_V:(h + 1) * MLA_V]
            acc_sc[h] += _dot(p.astype(BF16), vh)
        return carry

    lax.fori_loop(0, qi + 1, pv_body, 0)
    for h in range(MLA_HEADS):
        l = jnp.sum(l_sc[h], axis=-1, keepdims=True)
        o_ref[0, :, h * MLA_V:(h + 1) * MLA_V] = (acc_sc[h] / l).astype(BF16)


def _attention(q, k, v):
    bsz, seq, qk_w = q.shape
    tq = min(TQ_ATTN, seq)
    v_w = v.shape[-1]
    return pl.pallas_call(
        functools.partial(_attn_kernel, tq=tq),
        out_shape=jax.ShapeDtypeStruct((bsz, seq, v_w), BF16),
        grid=(bsz, seq // tq),
        in_specs=[pl.BlockSpec((1, tq, qk_w), lambda b, i: (b, i, 0)),
                  pl.BlockSpec((1, seq, qk_w), lambda b, i: (b, 0, 0)),
                  pl.BlockSpec((1, seq, v_w), lambda b, i: (b, 0, 0))],
        out_specs=pl.BlockSpec((1, tq, v_w), lambda b, i: (b, i, 0)),
        scratch_shapes=[pltpu.VMEM((seq // tq, MLA_HEADS, tq, tq), F32),
                        pltpu.VMEM((MLA_HEADS, tq, tq // 2), F32),
                        pltpu.VMEM((MLA_HEADS, tq, tq // 2), F32),
                        pltpu.VMEM((MLA_HEADS, tq, MLA_V), F32)],
        compiler_params=pltpu.CompilerParams(dimension_semantics=("arbitrary", "arbitrary"),
                                             vmem_limit_bytes=VMEM_LIMIT),
        name="mla_attention",
    )(q, k, v)


def _mix_back_kernel(x_ref, mp_ref, gm_ref, ym_ref, wbm_ref, wo_ref, nf_ref, wfi_ref, wfo_ref,
                     fn_ref, o_ref, *, final):
    merged = mp_ref[...].astype(F32) + gm_ref[...].astype(F32) * _dot(ym_ref[...], wbm_ref[...])
    x1 = x_ref[...] + _dot(merged.astype(BF16), wo_ref[...])
    hn = _rms(x1, nf_ref[...]).astype(BF16)
    acc = jnp.zeros_like(x1)
    for c in range(D_FF // FFN_COLS):
        g = _dot(hn, wfi_ref[:, c * FFN_COLS:(c + 1) * FFN_COLS])
        u = _dot(hn, wfi_ref[:, D_FF + c * FFN_COLS:D_FF + (c + 1) * FFN_COLS])
        act = (g * _sigmoid(g) * u).astype(BF16)
        acc = acc + _dot(act, wfo_ref[c * FFN_COLS:(c + 1) * FFN_COLS, :])
    x2 = x1 + acc
    if final:
        x2 = _rms(x2, fn_ref[...])
    o_ref[...] = x2


def _mix_back(x, mp, gm, ym, w_branch_m, w_out, norm_ffn, w_ffn_in, w_ffn_out, final_norm, final):
    t = x.shape[0]
    tm = min(TM_BACK, t)
    consts = [w_branch_m.astype(BF16), w_out.astype(BF16), norm_ffn.reshape(1, D_MODEL),
              w_ffn_in.astype(BF16), w_ffn_out.astype(BF16), final_norm.reshape(1, D_MODEL)]
    tile = lambda w: pl.BlockSpec((tm, w), lambda i: (i, 0))
    return pl.pallas_call(
        functools.partial(_mix_back_kernel, final=final),
        out_shape=jax.ShapeDtypeStruct((t, D_MODEL), F32),
        grid=(t // tm,),
        in_specs=[tile(D_MODEL), tile(D_MODEL), tile(D_MODEL), tile(BRANCH_W)]
                 + [_const_spec(c.shape) for c in consts],
        out_specs=tile(D_MODEL),
        compiler_params=pltpu.CompilerParams(dimension_semantics=("arbitrary",),
                                             vmem_limit_bytes=VMEM_LIMIT),
        name="mix_back_ffn",
    )(x, mp, gm, ym, *consts)


def kernel(x, positions, norm_mix, w_in, b_gate, conv_w, sg_ln_g, sg_ln_b, sg_ws, sg_b, mla_q_norm,
           mla_w_uq, mla_kv_norm, mla_w_ukv, w_branch, w_out, norm_ffn, w_ffn_in, w_ffn_out, final_norm):
    bsz, seq, d = x.shape
    depth = w_in.shape[0]
    tab = _rope_tables(positions)
    for l in range(depth):
        mp, gm, q, k, v = _mix_front(x, tab, norm_mix[l], w_in[l], b_gate[l], conv_w[l], sg_ln_g[l],
                                     sg_ln_b[l], sg_ws[l], sg_b[l], mla_q_norm[l], mla_w_uq[l],
                                     mla_kv_norm[l], mla_w_ukv[l], w_branch[l])
        ym = _attention(q, k, v)
        flat = lambda a: a.reshape(bsz * seq, a.shape[-1])
        x = _mix_back(flat(x), flat(mp), flat(gm), flat(ym), w_branch[l, 3], w_out[l], norm_ffn[l],
                      w_ffn_in[l], w_ffn_out[l], final_norm, final=(l == depth - 1)).reshape(bsz, seq, d)
    return x
```

```python
import functools

import numpy as np
import jax
import jax.numpy as jnp
from jax import lax
from jax.experimental import pallas as pl
from jax.experimental.pallas import tpu as pltpu

F32 = jnp.float32
BF16 = jnp.bfloat16

D_MODEL = 1024
N_BRANCH = 4
BRANCH_W = 512
EPS = 1e-6
ROPE_THETA = 10000.0
CHUNK = 128
CONV_DIM = 512
CONV_W = 3
RET_HEADS = 4
RET_DK = 64
RET_DV = 128
SG_DIM = 512
SG_GROUPS = 4
MLA_HEADS = 8
MLA_NOPE = 64
MLA_ROPE = 32
MLA_V = 64
MLA_Q_RANK = 384
MLA_KV_RANK = 256
D_FF = 2816

LANES = 128
MLA_HEAD_PAD = 128
CONV_HALO = 8
FFN_COLS = 256
VMEM_LIMIT = 56 * 1024 * 1024
QK_SCALE_LOG2E = float((MLA_NOPE + MLA_ROPE) ** -0.5 * np.log2(np.e))

TS_FRONT = 512
TQ_ATTN = 256
TM_BACK = 512
TS_TABLE = 512

_OFF = [int(v) for v in np.cumsum([0, 512, 512, 512, 256, 256, 512, 512, 512, 512, 384, 256, 32, 4096])]
W_MAIN_COLS = _OFF[9]
W_MLA_COLS = MLA_Q_RANK + MLA_KV_RANK + MLA_HEAD_PAD


def _dot(a, b):
    return jnp.dot(a, b, preferred_element_type=F32)


def _dot_nt(a, b):
    return lax.dot_general(a, b, (((1,), (1,)), ((), ())), preferred_element_type=F32)


def _rms(x, g):
    return x * lax.rsqrt(jnp.mean(x * x, axis=-1, keepdims=True) + EPS) * g


def _sigmoid(x):
    return 1.0 / (1.0 + jnp.exp(-x))


def _rot_half(x, half):
    n = x.shape[-1]
    lane = lax.broadcasted_iota(jnp.int32, x.shape, x.ndim - 1)
    fwd = pltpu.roll(x, n - half, axis=x.ndim - 1)
    bwd = pltpu.roll(x, half, axis=x.ndim - 1)
    return jnp.where((lane & (2 * half - 1)) < half, fwd, bwd)


def _layer_spec(arr, layer, lead=()):
    rest = arr.shape[1 + len(lead):]
    idx = (layer,) + tuple(lead) + (0,) * len(rest)
    return pl.BlockSpec((None,) * (1 + len(lead)) + tuple(rest), lambda *_: idx,
                        pipeline_mode=pl.Buffered(1))


def _const_spec(arr):
    nd = arr.ndim
    return pl.BlockSpec(arr.shape, lambda *_: (0,) * nd, pipeline_mode=pl.Buffered(1))


def _rope_table_kernel(pos_ref, inv_ref, tab_ref):
    pos = pos_ref[0]
    ang_r = pos * inv_ref[0:1, :]
    ang_m = pos * inv_ref[1:2, :]
    tab_ref[0, :, 0:128] = jnp.cos(ang_r)
    tab_ref[0, :, 128:256] = jnp.sin(ang_r) * inv_ref[2:3, :]
    tab_ref[0, :, 256:384] = jnp.cos(ang_m)
    tab_ref[0, :, 384:512] = jnp.sin(ang_m) * inv_ref[3:4, :]


def _rope_tables(positions):
    bsz, seq = positions.shape
    inv_r = ROPE_THETA ** (-jnp.arange(0, RET_DK, 2, dtype=F32) / RET_DK)
    inv_m = ROPE_THETA ** (-jnp.arange(0, MLA_ROPE, 2, dtype=F32) / MLA_ROPE)
    lane_r = jnp.tile(inv_r, LANES // inv_r.shape[0])
    pad = MLA_HEAD_PAD - MLA_NOPE - MLA_ROPE
    lane_m = jnp.concatenate([jnp.zeros((MLA_NOPE,), F32), inv_m, inv_m, jnp.zeros((pad,), F32)])
    sgn_r = np.tile(np.repeat(np.array([-1.0, 1.0], np.float32), RET_DK // 2), LANES // RET_DK)
    sgn_m = np.concatenate([np.zeros(MLA_NOPE, np.float32),
                            np.repeat(np.array([-1.0, 1.0], np.float32), MLA_ROPE // 2),
                            np.zeros(pad, np.float32)])
    inv = jnp.zeros((8, LANES), F32).at[0].set(lane_r).at[1].set(lane_m).at[2].set(sgn_r).at[3].set(sgn_m)
    pos_b = jnp.broadcast_to(positions.astype(F32)[:, :, None], (bsz, seq, LANES))
    ts = min(TS_TABLE, seq)
    return pl.pallas_call(
        _rope_table_kernel,
        out_shape=jax.ShapeDtypeStruct((bsz, seq, 4 * LANES), F32),
        grid=(bsz, seq // ts),
        in_specs=[pl.BlockSpec((1, ts, LANES), lambda b, s: (b, s, 0)),
                  pl.BlockSpec((8, LANES), lambda b, s: (0, 0))],
        out_specs=pl.BlockSpec((1, ts, 4 * LANES), lambda b, s: (b, s, 0)),
        compiler_params=pltpu.CompilerParams(dimension_semantics=("arbitrary", "arbitrary")),
        name="rope_tables",
    )(pos_b, inv)


def _mix_front_kernel(x_ref, tab_ref, nrm_ref, wmain_ref, wm_ref, wg_ref, bg_ref,
                      cw_ref, lng_ref, lnb_ref, sgw_ref, sgb_ref, qn_ref, wq_ref,
                      kvn_ref, wkk_ref, wkv_ref, wb_ref, dec_ref, xi_ref, zeta_ref,
                      mp_ref, gm_ref, q_ref, k_ref, v_ref,
                      conv_sc, state_sc, *, ts, chunk_decay):
    s_idx = pl.program_id(1)

    @pl.when(s_idx == 0)
    def _():
        conv_sc[0:CONV_HALO, :] = jnp.zeros((CONV_HALO, CONV_DIM), F32)
        state_sc[...] = jnp.zeros_like(state_sc)

    x = x_ref[0]
    hb = _rms(x, nrm_ref[...]).astype(BF16)
    tab = tab_ref[0]
    cr, sr = tab[:, 0:128], tab[:, 128:256]
    cm, sm = tab[:, 256:384], tab[:, 384:512]

    def gate(i):
        return _sigmoid(_dot(hb, wg_ref[:, i * D_MODEL:(i + 1) * D_MODEL]) + bg_ref[i:i + 1, :])

    pm = _dot(hb, wm_ref[...])
    cqn = _rms(pm[:, 0:MLA_Q_RANK], qn_ref[...]).astype(BF16)
    ckvn = _rms(pm[:, MLA_Q_RANK:MLA_Q_RANK + MLA_KV_RANK], kvn_ref[...]).astype(BF16)
    cm8 = jnp.concatenate([cm] * MLA_HEADS, axis=1)
    sm8 = jnp.concatenate([sm] * MLA_HEADS, axis=1)
    qa = _dot(cqn, wq_ref[...])
    qf = (qa * cm8 + _rot_half(qa, MLA_ROPE // 2) * sm8) * QK_SCALE_LOG2E
    q_ref[0] = qf.astype(BF16)
    kp = pm[:, MLA_Q_RANK + MLA_KV_RANK:W_MLA_COLS]
    kpe = kp * cm + _rot_half(kp, MLA_ROPE // 2) * sm
    kf = _dot(ckvn, wkk_ref[...]) + jnp.concatenate([kpe] * MLA_HEADS, axis=1)
    k_ref[0] = kf.astype(BF16)
    v_ref[0] = _dot(ckvn, wkv_ref[...]).astype(BF16)

    pr = _dot(hb, wmain_ref[:, 1536:3072])
    cr2 = jnp.concatenate([cr, cr], axis=1)
    sr2 = jnp.concatenate([sr, sr], axis=1)
    rq, rk = pr[:, 0:256], pr[:, 256:512]
    qr = rq * cr2 + _rot_half(rq, RET_DK // 2) * sr2
    kr = (rk * cr2 + _rot_half(rk, RET_DK // 2) * sr2) * (RET_DK ** -0.5)
    rv = pr[:, 512:1024].astype(BF16)
    rg = pr[:, 1024:1536]
    y_r_chunks = []
    gates = []
    n_chunks = ts // CHUNK
    for c in range(n_chunks):
        rows = slice(c * CHUNK, (c + 1) * CHUNK)
        qc, kc = qr[rows], kr[rows]
        qcb, kcb = qc.astype(BF16), kc.astype(BF16)
        qx = (qc * xi_ref[...]).astype(BF16)
        kz = kc * zeta_ref[...]
        heads = []
        for h in range(RET_HEADS):
            hs = slice(h * RET_DK, (h + 1) * RET_DK)
            vs = slice(h * RET_DV, (h + 1) * RET_DV)
            vh = rv[rows, vs]
            scores = _dot_nt(qcb[:, hs], kcb[:, hs]) * dec_ref[h]
            state = state_sc[h]
            o = _dot(scores.astype(BF16), vh) + _dot(qx[:, hs], state.astype(BF16))
            mu = jnp.mean(o, axis=-1, keepdims=True)
            oc = o - mu
            on = oc * lax.rsqrt(jnp.mean(oc * oc, axis=-1, keepdims=True) + EPS)
            g = rg[rows, vs]
            heads.append(g * _sigmoid(g) * on)
            kzt = jnp.transpose(kz[:, hs]).astype(BF16)
            state_sc[h] = chunk_decay[h] * state + _dot(kzt, vh)
        y_r_chunks.append(jnp.concatenate(heads, axis=1))
        for i in range(len(gates), (c + 1) * N_BRANCH // n_chunks):
            gates.append(gate(i))
    y_r = jnp.concatenate(y_r_chunks, axis=0)
    g_a, g_r, g_s, g_m = gates
    gm_ref[0] = g_m.astype(BF16)

    pa = _dot(hb, wmain_ref[:, 0:1536])
    a_b, a_c, a_x = pa[:, 0:512], pa[:, 512:1024], pa[:, 1024:1536]
    p = a_c * a_x
    conv_sc[CONV_HALO:CONV_HALO + ts, :] = p
    p1 = conv_sc[CONV_HALO - 1:CONV_HALO - 1 + ts, :]
    p2 = conv_sc[CONV_HALO - 2:CONV_HALO - 2 + ts, :]
    y_a = a_b * (cw_ref[0:1, :] * p2 + cw_ref[1:2, :] * p1 + cw_ref[2:3, :] * p)
    conv_sc[0:CONV_HALO, :] = p[ts - CONV_HALO:ts, :]
    mp = g_a * _dot(y_a.astype(BF16), wb_ref[0]) + g_r * _dot(y_r.astype(BF16), wb_ref[1])

    ps = _dot(hb, wmain_ref[:, 3072:4096])
    u = jax.nn.gelu(ps[:, 0:512])
    sv = jax.nn.gelu(ps[:, 512:1024])
    mu = jnp.mean(sv, axis=-1, keepdims=True)
    svc = sv - mu
    vln = svc * lax.rsqrt(jnp.mean(svc * svc, axis=-1, keepdims=True) + EPS) * lng_ref[...] + lnb_ref[...]
    vlnb = vln.astype(BF16)
    row = lax.broadcasted_iota(jnp.int32, (CHUNK, CHUNK), 0)
    col = lax.broadcasted_iota(jnp.int32, (CHUNK, CHUNK), 1)
    wtril = [jnp.where(row >= col, sgw_ref[g], 0.0).astype(BF16) for g in range(SG_GROUPS)]
    s_chunks = []
    for c in range(ts // CHUNK):
        rows = slice(c * CHUNK, (c + 1) * CHUNK)
        s_chunks.append(jnp.concatenate(
            [_dot(wtril[g], vlnb[rows, g * CHUNK:(g + 1) * CHUNK]) + sgb_ref[g]
             for g in range(SG_GROUPS)], axis=1))
    y_s = u * jnp.concatenate(s_chunks, axis=0)
    mp = mp + g_s * _dot(y_s.astype(BF16), wb_ref[2])
    mp_ref[0] = mp.astype(BF16)


def _retention_tables():
    log_gamma = np.log1p(-np.exp2(-5.0 - np.arange(RET_HEADS, dtype=np.float64)))
    idx = np.arange(CHUNK, dtype=np.float64)
    diff = idx[:, None] - idx[None, :]
    decay = np.where(diff >= 0, np.exp(np.maximum(diff, 0.0)[None] * log_gamma[:, None, None]), 0.0)
    zeta = np.exp((CHUNK - 1 - idx)[:, None] * log_gamma[None, :])
    xi = np.exp((idx + 1.0)[:, None] * log_gamma[None, :])
    zeta_full = np.repeat(zeta, RET_DK, axis=1)
    xi_full = np.repeat(xi, RET_DK, axis=1)
    chunk_decay = tuple(float(v) for v in np.exp(CHUNK * log_gamma))
    return (jnp.asarray(decay, F32), jnp.asarray(xi_full, F32), jnp.asarray(zeta_full, F32), chunk_decay)


def _prep_weights(w_in, mla_w_uq, mla_w_ukv, w_branch, w_out, w_ffn_in, w_ffn_out):
    depth = w_in.shape[0]
    o = _OFF
    wmain = w_in[:, :, :W_MAIN_COLS].astype(BF16)
    pad_l = jnp.zeros((depth, D_MODEL, MLA_NOPE), F32)
    pad_r = jnp.zeros((depth, D_MODEL, MLA_HEAD_PAD - MLA_NOPE - MLA_ROPE), F32)
    wm = jnp.concatenate([w_in[:, :, o[9]:o[11]], pad_l, w_in[:, :, o[11]:o[12]], pad_r], axis=-1).astype(BF16)
    wg = w_in[:, :, o[12]:o[13]].astype(BF16)
    dq = MLA_NOPE + MLA_ROPE
    uq = mla_w_uq.reshape(depth, MLA_Q_RANK, MLA_HEADS, dq)
    zq = jnp.zeros((depth, MLA_Q_RANK, MLA_HEADS, MLA_HEAD_PAD - dq), F32)
    wq = jnp.concatenate([uq, zq], axis=-1).reshape(depth, MLA_Q_RANK, MLA_HEADS * MLA_HEAD_PAD).astype(BF16)
    ukv = mla_w_ukv.reshape(depth, MLA_KV_RANK, MLA_HEADS, MLA_NOPE + MLA_V)
    zk = jnp.zeros((depth, MLA_KV_RANK, MLA_HEADS, MLA_HEAD_PAD - MLA_NOPE), F32)
    wkk = jnp.concatenate([ukv[..., :MLA_NOPE], zk], axis=-1).reshape(
        depth, MLA_KV_RANK, MLA_HEADS * MLA_HEAD_PAD).astype(BF16)
    wkv = ukv[..., MLA_NOPE:].reshape(depth, MLA_KV_RANK, MLA_HEADS * MLA_V).astype(BF16)
    return dict(wmain=wmain, wm=wm, wg=wg, wq=wq, wkk=wkk, wkv=wkv, wb=w_branch.astype(BF16),
                wo=w_out.astype(BF16), wfi=w_ffn_in.astype(BF16), wfo=w_ffn_out.astype(BF16))


def _mix_front(layer, x, tab, w, norm_mix, b_gate, conv_w, sg_ln_g, sg_ln_b, sg_ws, sgb,
               mla_q_norm, mla_kv_norm, ret_tabs):
    bsz, seq, _ = x.shape
    ts = min(TS_FRONT, seq)
    decay, xi_full, zeta_full, chunk_decay = ret_tabs
    stacked = [norm_mix, w["wmain"], w["wm"], w["wg"], b_gate, conv_w, sg_ln_g, sg_ln_b, sg_ws, sgb,
               mla_q_norm, w["wq"], mla_kv_norm, w["wkk"], w["wkv"], w["wb"]]
    consts = [decay, xi_full, zeta_full]
    tile = lambda wd: pl.BlockSpec((1, ts, wd), lambda b, s: (b, s, 0))
    qk_w = MLA_HEADS * MLA_HEAD_PAD
    out_shape = [jax.ShapeDtypeStruct((bsz, seq, wd), BF16)
                 for wd in (D_MODEL, D_MODEL, qk_w, qk_w, MLA_HEADS * MLA_V)]
    return pl.pallas_call(
        functools.partial(_mix_front_kernel, ts=ts, chunk_decay=chunk_decay),
        out_shape=out_shape,
        grid=(bsz, seq // ts),
        in_specs=[tile(D_MODEL), tile(4 * LANES)] + [_layer_spec(a, layer) for a in stacked]
                 + [_const_spec(c) for c in consts],
        out_specs=[tile(s.shape[-1]) for s in out_shape],
        scratch_shapes=[pltpu.VMEM((CONV_HALO + ts, CONV_DIM), F32),
                        pltpu.VMEM((RET_HEADS, RET_DK, RET_DV), F32)],
        compiler_params=pltpu.CompilerParams(dimension_semantics=("arbitrary", "arbitrary"),
                                             vmem_limit_bytes=VMEM_LIMIT),
        name="mix_front",
    )(x, tab, *stacked, *consts)


def _attn_kernel(q_ref, k_ref, v_ref, o_ref, s_sc, mx_sc, l_sc, acc_sc, *, tq):
    qi = pl.program_id(1)
    half = tq // 2
    mx_sc[...] = jnp.full(mx_sc.shape, -1e30, F32)
    l_sc[...] = jnp.zeros_like(l_sc)
    acc_sc[...] = jnp.zeros_like(acc_sc)

    def scores(j, masked):
        ks = pl.multiple_of(j * tq, tq)
        if masked:
            row = lax.broadcasted_iota(jnp.int32, (tq, tq), 0)
            col = lax.broadcasted_iota(jnp.int32, (tq, tq), 1)
            keep = col <= row
        for h in range(MLA_HEADS):
            hp = slice(h * MLA_HEAD_PAD, (h + 1) * MLA_HEAD_PAD)
            s = _dot_nt(q_ref[0, :, hp], k_ref[0, pl.ds(ks, tq), hp])
            if masked:
                s = jnp.where(keep, s, -1e30)
            s_sc[j, h] = s
            mx_sc[h] = jnp.maximum(mx_sc[h], jnp.maximum(s[:, :half], s[:, half:]))

    def scores_body(j, carry):
        scores(j, False)
        return carry

    lax.fori_loop(0, qi, scores_body, 0)
    scores(qi, True)

    for h in range(MLA_HEADS):
        mx_sc[h] = jnp.broadcast_to(jnp.max(mx_sc[h], axis=-1, keepdims=True), (tq, half))

    def pv_body(j, carry):
        ks = pl.multiple_of(j * tq, tq)
        for h in range(MLA_HEADS):
            m = mx_sc[h]
            p = jnp.exp2(s_sc[j, h] - jnp.concatenate([m, m], axis=1))
            l_sc[h] += p[:, :half] + p[:, half:]
            vh = v_ref[0, pl.ds(ks, tq), h * MLA_V:(h + 1) * MLA_V]
            acc_sc[h] += _dot(p.astype(BF16), vh)
        return carry

    lax.fori_loop(0, qi + 1, pv_body, 0)
    for h in range(MLA_HEADS):
        l = jnp.sum(l_sc[h], axis=-1, keepdims=True)
        o_ref[0, :, h * MLA_V:(h + 1) * MLA_V] = (acc_sc[h] / l).astype(BF16)


def _attention(q, k, v):
    bsz, seq, qk_w = q.shape
    tq = min(TQ_ATTN, seq)
    v_w = v.shape[-1]
    return pl.pallas_call(
        functools.partial(_attn_kernel, tq=tq),
        out_shape=jax.ShapeDtypeStruct((bsz, seq, v_w), BF16),
        grid=(bsz, seq // tq),
        in_specs=[pl.BlockSpec((1, tq, qk_w), lambda b, i: (b, i, 0)),
                  pl.BlockSpec((1, seq, qk_w), lambda b, i: (b, 0, 0)),
                  pl.BlockSpec((1, seq, v_w), lambda b, i: (b, 0, 0))],
        out_specs=pl.BlockSpec((1, tq, v_w), lambda b, i: (b, i, 0)),
        scratch_shapes=[pltpu.VMEM((seq // tq, MLA_HEADS, tq, tq), F32),
                        pltpu.VMEM((MLA_HEADS, tq, tq // 2), F32),
                        pltpu.VMEM((MLA_HEADS, tq, tq // 2), F32),
                        pltpu.VMEM((MLA_HEADS, tq, MLA_V), F32)],
        compiler_params=pltpu.CompilerParams(dimension_semantics=("arbitrary", "arbitrary"),
                                             vmem_limit_bytes=VMEM_LIMIT),
        name="mla_attention",
    )(q, k, v)


def _mix_back_kernel(x_ref, mp_ref, gm_ref, ym_ref, wbm_ref, wo_ref, nf_ref, wfi_ref, wfo_ref,
                     fn_ref, o_ref, *, final):
    merged = mp_ref[...].astype(F32) + gm_ref[...].astype(F32) * _dot(ym_ref[...], wbm_ref[...])
    x1 = x_ref[...] + _dot(merged.astype(BF16), wo_ref[...])
    hn = _rms(x1, nf_ref[...]).astype(BF16)
    acc = jnp.zeros_like(x1)
    for c in range(D_FF // FFN_COLS):
        g = _dot(hn, wfi_ref[:, c * FFN_COLS:(c + 1) * FFN_COLS])
        u = _dot(hn, wfi_ref[:, D_FF + c * FFN_COLS:D_FF + (c + 1) * FFN_COLS])
        act = (g * _sigmoid(g) * u).astype(BF16)
        acc = acc + _dot(act, wfo_ref[c * FFN_COLS:(c + 1) * FFN_COLS, :])
    x2 = x1 + acc
    if final:
        x2 = _rms(x2, fn_ref[...])
    o_ref[...] = x2


def _mix_back(layer, x, mp, gm, ym, w, norm_ffn, final_norm, final):
    t = x.shape[0]
    tm = min(TM_BACK, t)
    tile = lambda wd: pl.BlockSpec((tm, wd), lambda i: (i, 0))
    return pl.pallas_call(
        functools.partial(_mix_back_kernel, final=final),
        out_shape=jax.ShapeDtypeStruct((t, D_MODEL), F32),
        grid=(t // tm,),
        in_specs=[tile(D_MODEL), tile(D_MODEL), tile(D_MODEL), tile(BRANCH_W),
                  _layer_spec(w["wb"], layer, lead=(N_BRANCH - 1,)), _layer_spec(w["wo"], layer),
                  _layer_spec(norm_ffn, layer), _layer_spec(w["wfi"], layer), _layer_spec(w["wfo"], layer),
                  _const_spec(final_norm)],
        out_specs=tile(D_MODEL),
        compiler_params=pltpu.CompilerParams(dimension_semantics=("arbitrary",),
                                             vmem_limit_bytes=VMEM_LIMIT),
        name="mix_back_ffn",
    )(x, mp, gm, ym, w["wb"], w["wo"], norm_ffn, w["wfi"], w["wfo"], final_norm)


def kernel(x, positions, norm_mix, w_in, b_gate, conv_w, sg_ln_g, sg_ln_b, sg_ws, sg_b, mla_q_norm,
           mla_w_uq, mla_kv_norm, mla_w_ukv, w_branch, w_out, norm_ffn, w_ffn_in, w_ffn_out, final_norm):
    bsz, seq, d = x.shape
    depth = w_in.shape[0]
    tab = _rope_tables(positions)
    w = _prep_weights(w_in, mla_w_uq, mla_w_ukv, w_branch, w_out, w_ffn_in, w_ffn_out)
    ret_tabs = _retention_tables()
    row = lambda a: a.reshape(depth, 1, a.shape[-1])
    sgb = jnp.broadcast_to(sg_b[:, :, :, None], (depth, SG_GROUPS, CHUNK, CHUNK))
    norm_mix, sg_ln_g, sg_ln_b = row(norm_mix), row(sg_ln_g), row(sg_ln_b)
    mla_q_norm, mla_kv_norm, norm_ffn = row(mla_q_norm), row(mla_kv_norm), row(norm_ffn)
    final_norm = final_norm.reshape(1, d)
    flat = lambda a: a.reshape(bsz * seq, a.shape[-1])
    for l in range(depth):
        mp, gm, q, k, v = _mix_front(l, x, tab, w, norm_mix, b_gate, conv_w, sg_ln_g, sg_ln_b, sg_ws, sgb,
                                     mla_q_norm, mla_kv_norm, ret_tabs)
        ym = _attention(q, k, v)
        x = _mix_back(l, flat(x), flat(mp), flat(gm), flat(ym), w, norm_ffn, final_norm,
                      final=(l == depth - 1)).reshape(bsz, seq, d)
    return x
```

```python
import functools

import numpy as np
import jax
import jax.numpy as jnp
from jax import lax
from jax.experimental import pallas as pl
from jax.experimental.pallas import tpu as pltpu

F32 = jnp.float32
BF16 = jnp.bfloat16

D_MODEL = 1024
N_BRANCH = 4
BRANCH_W = 512
EPS = 1e-6
ROPE_THETA = 10000.0
CHUNK = 128
CONV_DIM = 512
CONV_W = 3
RET_HEADS = 4
RET_DK = 64
RET_DV = 128
SG_DIM = 512
SG_GROUPS = 4
MLA_HEADS = 8
MLA_NOPE = 64
MLA_ROPE = 32
MLA_V = 64
MLA_Q_RANK = 384
MLA_KV_RANK = 256
D_FF = 2816

LANES = 128
MLA_HEAD_PAD = 128
CONV_HALO = 8
FFN_COLS = 256
VMEM_LIMIT = 60 * 1024 * 1024
QK_SCALE_LOG2E = float((MLA_NOPE + MLA_ROPE) ** -0.5 * np.log2(np.e))

TS_FRONT = 512
TQ_ATTN = 256
TM_BACK = 512
TS_TABLE = 512

_OFF = [int(v) for v in np.cumsum([0, 512, 512, 512, 256, 256, 512, 512, 512, 512, 384, 256, 32, 4096])]
W_MAIN_COLS = _OFF[9]
W_MLA_COLS = MLA_Q_RANK + MLA_KV_RANK + MLA_HEAD_PAD
GATE_ORDER = (2, 3, 0, 1)


def _dot(a, b):
    return jnp.dot(a, b, preferred_element_type=F32)


def _dot_nt(a, b):
    return lax.dot_general(a, b, (((1,), (1,)), ((), ())), preferred_element_type=F32)


def _rms(x, g):
    return x * lax.rsqrt(jnp.mean(x * x, axis=-1, keepdims=True) + EPS) * g


def _sigmoid(x):
    return 1.0 / (1.0 + jnp.exp(-x))


def _rot_half(x, half):
    n = x.shape[-1]
    lane = lax.broadcasted_iota(jnp.int32, x.shape, x.ndim - 1)
    fwd = pltpu.roll(x, n - half, axis=x.ndim - 1)
    bwd = pltpu.roll(x, half, axis=x.ndim - 1)
    return jnp.where((lane & (2 * half - 1)) < half, fwd, bwd)


def _layer_spec(arr, layer, lead=()):
    rest = arr.shape[1 + len(lead):]
    idx = (layer,) + tuple(lead) + (0,) * len(rest)
    return pl.BlockSpec((None,) * (1 + len(lead)) + tuple(rest), lambda *_: idx,
                        pipeline_mode=pl.Buffered(1))


def _const_spec(arr):
    nd = arr.ndim
    return pl.BlockSpec(arr.shape, lambda *_: (0,) * nd, pipeline_mode=pl.Buffered(1))


def _rope_table_kernel(pos_ref, inv_ref, tab_ref):
    pos = pos_ref[0]
    ang_r = pos * inv_ref[0:1, :]
    ang_m = pos * inv_ref[1:2, :]
    tab_ref[0, :, 0:128] = jnp.cos(ang_r)
    tab_ref[0, :, 128:256] = jnp.sin(ang_r) * inv_ref[2:3, :]
    tab_ref[0, :, 256:384] = jnp.cos(ang_m)
    tab_ref[0, :, 384:512] = jnp.sin(ang_m) * inv_ref[3:4, :]


def _rope_tables(positions):
    bsz, seq = positions.shape
    inv_r = ROPE_THETA ** (-jnp.arange(0, RET_DK, 2, dtype=F32) / RET_DK)
    inv_m = ROPE_THETA ** (-jnp.arange(0, MLA_ROPE, 2, dtype=F32) / MLA_ROPE)
    lane_r = jnp.tile(inv_r, LANES // inv_r.shape[0])
    pad = MLA_HEAD_PAD - MLA_NOPE - MLA_ROPE
    lane_m = jnp.concatenate([jnp.zeros((MLA_NOPE,), F32), inv_m, inv_m, jnp.zeros((pad,), F32)])
    sgn_r = np.tile(np.repeat(np.array([-1.0, 1.0], np.float32), RET_DK // 2), LANES // RET_DK)
    sgn_m = np.concatenate([np.zeros(MLA_NOPE, np.float32),
                            np.repeat(np.array([-1.0, 1.0], np.float32), MLA_ROPE // 2),
                            np.zeros(pad, np.float32)])
    inv = jnp.zeros((8, LANES), F32).at[0].set(lane_r).at[1].set(lane_m).at[2].set(sgn_r).at[3].set(sgn_m)
    pos_b = jnp.broadcast_to(positions.astype(F32)[:, :, None], (bsz, seq, LANES))
    ts = min(TS_TABLE, seq)
    return pl.pallas_call(
        _rope_table_kernel,
        out_shape=jax.ShapeDtypeStruct((bsz, seq, 4 * LANES), F32),
        grid=(bsz, seq // ts),
        in_specs=[pl.BlockSpec((1, ts, LANES), lambda b, s: (b, s, 0)),
                  pl.BlockSpec((8, LANES), lambda b, s: (0, 0))],
        out_specs=pl.BlockSpec((1, ts, 4 * LANES), lambda b, s: (b, s, 0)),
        compiler_params=pltpu.CompilerParams(dimension_semantics=("arbitrary", "arbitrary")),
        name="rope_tables",
    )(pos_b, inv)


def _mix_front_kernel(x_ref, tab_ref, nrm_ref, wmain_ref, wm_ref, wg_ref, bg_ref,
                      cw_ref, lng_ref, lnb_ref, sgw_ref, sgb_ref, qn_ref, wq_ref,
                      kvn_ref, wkk_ref, wkv_ref, wb_ref, dec_ref, xi_ref, zeta_ref,
                      mp_ref, gm_ref, q_ref, k_ref, v_ref,
                      conv_sc, state_sc, *, ts, chunk_decay):
    s_idx = pl.program_id(1)

    @pl.when(s_idx == 0)
    def _():
        conv_sc[0:CONV_HALO, :] = jnp.zeros((CONV_HALO, CONV_DIM), F32)
        state_sc[...] = jnp.zeros_like(state_sc)

    x = x_ref[0]
    hb = _rms(x, nrm_ref[...]).astype(BF16)
    tab = tab_ref[0]
    cr, sr = tab[:, 0:128], tab[:, 128:256]
    cm, sm = tab[:, 256:384], tab[:, 384:512]

    def gate(i):
        return _sigmoid(_dot(hb, wg_ref[:, i * D_MODEL:(i + 1) * D_MODEL]) + bg_ref[i:i + 1, :])

    ps = _dot(hb, wmain_ref[:, 3072:4096])
    u = jax.nn.gelu(ps[:, 0:512])
    sv = jax.nn.gelu(ps[:, 512:1024])
    mu = jnp.mean(sv, axis=-1, keepdims=True)
    svc = sv - mu
    vln = svc * lax.rsqrt(jnp.mean(svc * svc, axis=-1, keepdims=True) + EPS) * lng_ref[...] + lnb_ref[...]
    vlnb = vln.astype(BF16)
    row = lax.broadcasted_iota(jnp.int32, (CHUNK, CHUNK), 0)
    col = lax.broadcasted_iota(jnp.int32, (CHUNK, CHUNK), 1)
    wtril = [jnp.where(row >= col, sgw_ref[g], 0.0).astype(BF16) for g in range(SG_GROUPS)]
    s_chunks = []
    for c in range(ts // CHUNK):
        rows = slice(c * CHUNK, (c + 1) * CHUNK)
        s_chunks.append(jnp.concatenate(
            [_dot(wtril[g], vlnb[rows, g * CHUNK:(g + 1) * CHUNK]) + sgb_ref[g]
             for g in range(SG_GROUPS)], axis=1))
    y_s = (u * jnp.concatenate(s_chunks, axis=0)).astype(BF16)

    pm = _dot(hb, wm_ref[...])
    cqn = _rms(pm[:, 0:MLA_Q_RANK], qn_ref[...]).astype(BF16)
    ckvn = _rms(pm[:, MLA_Q_RANK:MLA_Q_RANK + MLA_KV_RANK], kvn_ref[...]).astype(BF16)
    cm8 = jnp.concatenate([cm] * MLA_HEADS, axis=1)
    sm8 = jnp.concatenate([sm] * MLA_HEADS, axis=1)
    qa = _dot(cqn, wq_ref[...])
    qf = (qa * cm8 + _rot_half(qa, MLA_ROPE // 2) * sm8) * QK_SCALE_LOG2E
    q_ref[0] = qf.astype(BF16)
    kp = pm[:, MLA_Q_RANK + MLA_KV_RANK:W_MLA_COLS]
    kpe = kp * cm + _rot_half(kp, MLA_ROPE // 2) * sm
    kf = _dot(ckvn, wkk_ref[...]) + jnp.concatenate([kpe] * MLA_HEADS, axis=1)
    k_ref[0] = kf.astype(BF16)
    lane = lax.broadcasted_iota(jnp.int32, (1, MLA_HEADS * MLA_HEAD_PAD), 1)
    ones_col = ((lane & (MLA_HEAD_PAD - 1)) == MLA_V).astype(F32)
    v_ref[0] = (_dot(ckvn, wkv_ref[...]) + ones_col).astype(BF16)

    pr = _dot(hb, wmain_ref[:, 1536:3072])
    cr2 = jnp.concatenate([cr, cr], axis=1)
    sr2 = jnp.concatenate([sr, sr], axis=1)
    rq, rk = pr[:, 0:256], pr[:, 256:512]
    qr = rq * cr2 + _rot_half(rq, RET_DK // 2) * sr2
    kr = (rk * cr2 + _rot_half(rk, RET_DK // 2) * sr2) * (RET_DK ** -0.5)
    rv = pr[:, 512:1024].astype(BF16)
    rg = pr[:, 1024:1536]
    y_r_chunks = []
    gates = []
    n_chunks = ts // CHUNK
    for c in range(n_chunks):
        rows = slice(c * CHUNK, (c + 1) * CHUNK)
        qc, kc = qr[rows], kr[rows]
        qcb, kcb = qc.astype(BF16), kc.astype(BF16)
        qx = (qc * xi_ref[...]).astype(BF16)
        kz = kc * zeta_ref[...]
        heads = []
        for h in range(RET_HEADS):
            hs = slice(h * RET_DK, (h + 1) * RET_DK)
            vs = slice(h * RET_DV, (h + 1) * RET_DV)
            vh = rv[rows, vs]
            scores = _dot_nt(qcb[:, hs], kcb[:, hs]) * dec_ref[h]
            state = state_sc[h]
            o = _dot(scores.astype(BF16), vh) + _dot(qx[:, hs], state.astype(BF16))
            mu = jnp.mean(o, axis=-1, keepdims=True)
            oc = o - mu
            on = oc * lax.rsqrt(jnp.mean(oc * oc, axis=-1, keepdims=True) + EPS)
            g = rg[rows, vs]
            heads.append(g * _sigmoid(g) * on)
            kzt = jnp.transpose(kz[:, hs]).astype(BF16)
            state_sc[h] = chunk_decay[h] * state + _dot(kzt, vh)
        y_r_chunks.append(jnp.concatenate(heads, axis=1))
        for i in range(len(gates), (c + 1) * N_BRANCH // n_chunks):
            gates.append(gate(GATE_ORDER[i]))
    y_r = jnp.concatenate(y_r_chunks, axis=0)
    g_s, g_m, g_a, g_r = gates
    gm_ref[0] = g_m.astype(BF16)
    mp = g_s * _dot(y_s, wb_ref[2])

    pa = _dot(hb, wmain_ref[:, 0:1536])
    a_b, a_c, a_x = pa[:, 0:512], pa[:, 512:1024], pa[:, 1024:1536]
    p = a_c * a_x
    conv_sc[CONV_HALO:CONV_HALO + ts, :] = p
    p1 = conv_sc[CONV_HALO - 1:CONV_HALO - 1 + ts, :]
    p2 = conv_sc[CONV_HALO - 2:CONV_HALO - 2 + ts, :]
    y_a = a_b * (cw_ref[0:1, :] * p2 + cw_ref[1:2, :] * p1 + cw_ref[2:3, :] * p)
    conv_sc[0:CONV_HALO, :] = p[ts - CONV_HALO:ts, :]
    mp = mp + g_a * _dot(y_a.astype(BF16), wb_ref[0]) + g_r * _dot(y_r.astype(BF16), wb_ref[1])
    mp_ref[0] = mp.astype(BF16)


def _retention_tables():
    log_gamma = np.log1p(-np.exp2(-5.0 - np.arange(RET_HEADS, dtype=np.float64)))
    idx = np.arange(CHUNK, dtype=np.float64)
    diff = idx[:, None] - idx[None, :]
    decay = np.where(diff >= 0, np.exp(np.maximum(diff, 0.0)[None] * log_gamma[:, None, None]), 0.0)
    zeta = np.exp((CHUNK - 1 - idx)[:, None] * log_gamma[None, :])
    xi = np.exp((idx + 1.0)[:, None] * log_gamma[None, :])
    zeta_full = np.repeat(zeta, RET_DK, axis=1)
    xi_full = np.repeat(xi, RET_DK, axis=1)
    chunk_decay = tuple(float(v) for v in np.exp(CHUNK * log_gamma))
    return (jnp.asarray(decay, F32), jnp.asarray(xi_full, F32), jnp.asarray(zeta_full, F32), chunk_decay)


def _prep_weights(w_in, mla_w_uq, mla_w_ukv, w_branch, w_out, w_ffn_in, w_ffn_out):
    depth = w_in.shape[0]
    o = _OFF
    wmain = w_in[:, :, :W_MAIN_COLS].astype(BF16)
    pad_l = jnp.zeros((depth, D_MODEL, MLA_NOPE), F32)
    pad_r = jnp.zeros((depth, D_MODEL, MLA_HEAD_PAD - MLA_NOPE - MLA_ROPE), F32)
    wm = jnp.concatenate([w_in[:, :, o[9]:o[11]], pad_l, w_in[:, :, o[11]:o[12]], pad_r], axis=-1).astype(BF16)
    wg = w_in[:, :, o[12]:o[13]].astype(BF16)
    dq = MLA_NOPE + MLA_ROPE
    uq = mla_w_uq.reshape(depth, MLA_Q_RANK, MLA_HEADS, dq)
    zq = jnp.zeros((depth, MLA_Q_RANK, MLA_HEADS, MLA_HEAD_PAD - dq), F32)
    wq = jnp.concatenate([uq, zq], axis=-1).reshape(depth, MLA_Q_RANK, MLA_HEADS * MLA_HEAD_PAD).astype(BF16)
    ukv = mla_w_ukv.reshape(depth, MLA_KV_RANK, MLA_HEADS, MLA_NOPE + MLA_V)
    zk = jnp.zeros((depth, MLA_KV_RANK, MLA_HEADS, MLA_HEAD_PAD - MLA_NOPE), F32)
    wkk = jnp.concatenate([ukv[..., :MLA_NOPE], zk], axis=-1).reshape(
        depth, MLA_KV_RANK, MLA_HEADS * MLA_HEAD_PAD).astype(BF16)
    zv = jnp.zeros((depth, MLA_KV_RANK, MLA_HEADS, MLA_HEAD_PAD - MLA_V), F32)
    wkv = jnp.concatenate([ukv[..., MLA_NOPE:], zv], axis=-1).reshape(
        depth, MLA_KV_RANK, MLA_HEADS * MLA_HEAD_PAD).astype(BF16)
    return dict(wmain=wmain, wm=wm, wg=wg, wq=wq, wkk=wkk, wkv=wkv, wb=w_branch.astype(BF16),
                wo=w_out.astype(BF16), wfi=w_ffn_in.astype(BF16), wfo=w_ffn_out.astype(BF16))


def _mix_front(layer, x, tab, w, norm_mix, b_gate, conv_w, sg_ln_g, sg_ln_b, sg_ws, sgb,
               mla_q_norm, mla_kv_norm, ret_tabs):
    bsz, seq, _ = x.shape
    ts = min(TS_FRONT, seq)
    decay, xi_full, zeta_full, chunk_decay = ret_tabs
    stacked = [norm_mix, w["wmain"], w["wm"], w["wg"], b_gate, conv_w, sg_ln_g, sg_ln_b, sg_ws, sgb,
               mla_q_norm, w["wq"], mla_kv_norm, w["wkk"], w["wkv"], w["wb"]]
    consts = [decay, xi_full, zeta_full]
    tile = lambda wd: pl.BlockSpec((1, ts, wd), lambda b, s: (b, s, 0))
    qk_w = MLA_HEADS * MLA_HEAD_PAD
    out_shape = [jax.ShapeDtypeStruct((bsz, seq, wd), BF16)
                 for wd in (D_MODEL, D_MODEL, qk_w, qk_w, qk_w)]
    return pl.pallas_call(
        functools.partial(_mix_front_kernel, ts=ts, chunk_decay=chunk_decay),
        out_shape=out_shape,
        grid=(bsz, seq // ts),
        in_specs=[tile(D_MODEL), tile(4 * LANES)] + [_layer_spec(a, layer) for a in stacked]
                 + [_const_spec(c) for c in consts],
        out_specs=[tile(s.shape[-1]) for s in out_shape],
        scratch_shapes=[pltpu.VMEM((CONV_HALO + ts, CONV_DIM), F32),
                        pltpu.VMEM((RET_HEADS, RET_DK, RET_DV), F32)],
        compiler_params=pltpu.CompilerParams(dimension_semantics=("arbitrary", "arbitrary"),
                                             vmem_limit_bytes=VMEM_LIMIT),
        name="mix_front",
    )(x, tab, *stacked, *consts)


def _attn_kernel(q_ref, k_ref, v_ref, o_ref, s_sc, mx_sc, acc_sc, *, tq):
    qi = pl.program_id(1)
    half = tq // 2
    n_pairs = qi // 2
    heads = range(MLA_HEADS)
    hp = lambda h: slice(h * MLA_HEAD_PAD, (h + 1) * MLA_HEAD_PAD)

    def lane_max(s):
        parts = [s[:, i * half:(i + 1) * half] for i in range(s.shape[1] // half)]
        while len(parts) > 1:
            parts = [jnp.maximum(a, b) for a, b in zip(parts[0::2], parts[1::2])]
        return parts[0]

    row = lax.broadcasted_iota(jnp.int32, (tq, tq), 0)
    col = lax.broadcasted_iota(jnp.int32, (tq, tq), 1)
    kd = pl.multiple_of(qi * tq, tq)
    for h in heads:
        s = _dot_nt(q_ref[0, :, hp(h)], k_ref[0, pl.ds(kd, tq), hp(h)])
        s = jnp.where(col <= row, s, -1e30)
        s_sc[qi, h] = s
        mx_sc[h] = lane_max(s)

    def scores_pair(i, carry):
        ks = pl.multiple_of(i * 2 * tq, 2 * tq)
        for h in heads:
            s = _dot_nt(q_ref[0, :, hp(h)], k_ref[0, pl.ds(ks, 2 * tq), hp(h)])
            s_sc[2 * i, h] = s[:, :tq]
            s_sc[2 * i + 1, h] = s[:, tq:]
            mx_sc[h] = jnp.maximum(mx_sc[h], lane_max(s))
        return carry

    lax.fori_loop(0, n_pairs, scores_pair, 0)

    @pl.when(qi % 2 == 1)
    def _():
        ks = pl.multiple_of((qi - 1) * tq, tq)
        for h in heads:
            s = _dot_nt(q_ref[0, :, hp(h)], k_ref[0, pl.ds(ks, tq), hp(h)])
            s_sc[qi - 1, h] = s
            mx_sc[h] = jnp.maximum(mx_sc[h], lane_max(s))

    for h in heads:
        mx_sc[h] = jnp.broadcast_to(jnp.max(mx_sc[h], axis=-1, keepdims=True), (tq, half))

    def probs(blocks, h):
        m = mx_sc[h]
        m2 = jnp.concatenate([m, m], axis=1)
        return jnp.concatenate([jnp.exp2(s_sc[j, h] - m2) for j in blocks], axis=1).astype(BF16)

    for h in heads:
        acc_sc[h] = _dot(probs([qi], h), v_ref[0, pl.ds(kd, tq), hp(h)])

    def pv_pair(i, carry):
        ks = pl.multiple_of(i * 2 * tq, 2 * tq)
        for h in heads:
            acc_sc[h] += _dot(probs([2 * i, 2 * i + 1], h), v_ref[0, pl.ds(ks, 2 * tq), hp(h)])
        return carry

    lax.fori_loop(0, n_pairs, pv_pair, 0)

    @pl.when(qi % 2 == 1)
    def _():
        ks = pl.multiple_of((qi - 1) * tq, tq)
        for h in heads:
            acc_sc[h] += _dot(probs([qi - 1], h), v_ref[0, pl.ds(ks, tq), hp(h)])

    for h2 in range(MLA_HEADS // 2):
        outs = []
        for h in (2 * h2, 2 * h2 + 1):
            acc = acc_sc[h]
            outs.append(acc[:, :MLA_V] / acc[:, MLA_V:MLA_V + 1])
        o_ref[0, :, h2 * 2 * MLA_V:(h2 + 1) * 2 * MLA_V] = jnp.concatenate(outs, axis=1).astype(BF16)


def _attention(q, k, v):
    bsz, seq, qk_w = q.shape
    tq = min(TQ_ATTN, seq)
    o_w = MLA_HEADS * MLA_V
    return pl.pallas_call(
        functools.partial(_attn_kernel, tq=tq),
        out_shape=jax.ShapeDtypeStruct((bsz, seq, o_w), BF16),
        grid=(bsz, seq // tq),
        in_specs=[pl.BlockSpec((1, tq, qk_w), lambda b, i: (b, i, 0)),
                  pl.BlockSpec((1, seq, qk_w), lambda b, i: (b, 0, 0)),
                  pl.BlockSpec((1, seq, qk_w), lambda b, i: (b, 0, 0))],
        out_specs=pl.BlockSpec((1, tq, o_w), lambda b, i: (b, i, 0)),
        scratch_shapes=[pltpu.VMEM((seq // tq, MLA_HEADS, tq, tq), F32),
                        pltpu.VMEM((MLA_HEADS, tq, tq // 2), F32),
                        pltpu.VMEM((MLA_HEADS, tq, MLA_HEAD_PAD), F32)],
        compiler_params=pltpu.CompilerParams(dimension_semantics=("arbitrary", "arbitrary"),
                                             vmem_limit_bytes=VMEM_LIMIT),
        name="mla_attention",
    )(q, k, v)


def _mix_back_kernel(x_ref, mp_ref, gm_ref, ym_ref, wbm_ref, wo_ref, nf_ref, wfi_ref, wfo_ref,
                     fn_ref, o_ref, *, final):
    merged = mp_ref[...].astype(F32) + gm_ref[...].astype(F32) * _dot(ym_ref[...], wbm_ref[...])
    x1 = x_ref[...] + _dot(merged.astype(BF16), wo_ref[...])
    hn = _rms(x1, nf_ref[...]).astype(BF16)
    acc = jnp.zeros_like(x1)
    for c in range(D_FF // FFN_COLS):
        g = _dot(hn, wfi_ref[:, c * FFN_COLS:(c + 1) * FFN_COLS])
        u = _dot(hn, wfi_ref[:, D_FF + c * FFN_COLS:D_FF + (c + 1) * FFN_COLS])
        act = (g * _sigmoid(g) * u).astype(BF16)
        acc = acc + _dot(act, wfo_ref[c * FFN_COLS:(c + 1) * FFN_COLS, :])
    x2 = x1 + acc
    if final:
        x2 = _rms(x2, fn_ref[...])
    o_ref[...] = x2


def _mix_back(layer, x, mp, gm, ym, w, norm_ffn, final_norm, final):
    t = x.shape[0]
    tm = min(TM_BACK, t)
    tile = lambda wd: pl.BlockSpec((tm, wd), lambda i: (i, 0))
    return pl.pallas_call(
        functools.partial(_mix_back_kernel, final=final),
        out_shape=jax.ShapeDtypeStruct((t, D_MODEL), F32),
        grid=(t // tm,),
        in_specs=[tile(D_MODEL), tile(D_MODEL), tile(D_MODEL), tile(BRANCH_W),
                  _layer_spec(w["wb"], layer, lead=(N_BRANCH - 1,)), _layer_spec(w["wo"], layer),
                  _layer_spec(norm_ffn, layer), _layer_spec(w["wfi"], layer), _layer_spec(w["wfo"], layer),
                  _const_spec(final_norm)],
        out_specs=tile(D_MODEL),
        compiler_params=pltpu.CompilerParams(dimension_semantics=("arbitrary",),
                                             vmem_limit_bytes=VMEM_LIMIT),
        name="mix_back_ffn",
    )(x, mp, gm, ym, w["wb"], w["wo"], norm_ffn, w["wfi"], w["wfo"], final_norm)


def kernel(x, positions, norm_mix, w_in, b_gate, conv_w, sg_ln_g, sg_ln_b, sg_ws, sg_b, mla_q_norm,
           mla_w_uq, mla_kv_norm, mla_w_ukv, w_branch, w_out, norm_ffn, w_ffn_in, w_ffn_out, final_norm):
    bsz, seq, d = x.shape
    depth = w_in.shape[0]
    tab = _rope_tables(positions)
    w = _prep_weights(w_in, mla_w_uq, mla_w_ukv, w_branch, w_out, w_ffn_in, w_ffn_out)
    ret_tabs = _retention_tables()
    row = lambda a: a.reshape(depth, 1, a.shape[-1])
    sgb = jnp.broadcast_to(sg_b[:, :, :, None], (depth, SG_GROUPS, CHUNK, CHUNK))
    norm_mix, sg_ln_g, sg_ln_b = row(norm_mix), row(sg_ln_g), row(sg_ln_b)
    mla_q_norm, mla_kv_norm, norm_ffn = row(mla_q_norm), row(mla_kv_norm), row(norm_ffn)
    final_norm = final_norm.reshape(1, d)
    flat = lambda a: a.reshape(bsz * seq, a.shape[-1])
    for l in range(depth):
        mp, gm, q, k, v = _mix_front(l, x, tab, w, norm_mix, b_gate, conv_w, sg_ln_g, sg_ln_b, sg_ws, sgb,
                                     mla_q_norm, mla_kv_norm, ret_tabs)
        ym = _attention(q, k, v)
        x = _mix_back(l, flat(x), flat(mp), flat(gm), flat(ym), w, norm_ffn, final_norm,
                      final=(l == depth - 1)).reshape(bsz, seq, d)
    return x
```

```python
import functools

import numpy as np
import jax
import jax.numpy as jnp
from jax import lax
from jax.experimental import pallas as pl
from jax.experimental.pallas import tpu as pltpu

F32 = jnp.float32
BF16 = jnp.bfloat16

D_MODEL = 1024
N_BRANCH = 4
BRANCH_W = 512
EPS = 1e-6
ROPE_THETA = 10000.0
CHUNK = 128
CONV_DIM = 512
CONV_W = 3
RET_HEADS = 4
RET_DK = 64
RET_DV = 128
SG_DIM = 512
SG_GROUPS = 4
MLA_HEADS = 8
MLA_NOPE = 64
MLA_ROPE = 32
MLA_V = 64
MLA_Q_RANK = 384
MLA_KV_RANK = 256
D_FF = 2816

LANES = 128
MLA_HEAD_PAD = 128
CONV_HALO = 8
FFN_COLS = 256
VMEM_LIMIT = 60 * 1024 * 1024
QK_SCALE_LOG2E = float((MLA_NOPE + MLA_ROPE) ** -0.5 * np.log2(np.e))

TS_FRONT = 512
TQ_ATTN = 256
TM_BACK = 512
TS_TABLE = 512

_OFF = [int(v) for v in np.cumsum([0, 512, 512, 512, 256, 256, 512, 512, 512, 512, 384, 256, 32, 4096])]
W_MAIN_COLS = _OFF[9]
W_MLA_COLS = MLA_Q_RANK + MLA_KV_RANK + MLA_HEAD_PAD
GATE_ORDER = (2, 3, 0, 1)


def _dot(a, b):
    return jnp.dot(a, b, preferred_element_type=F32)


def _dot_nt(a, b):
    return lax.dot_general(a, b, (((1,), (1,)), ((), ())), preferred_element_type=F32)


def _rms(x, g):
    return x * lax.rsqrt(jnp.mean(x * x, axis=-1, keepdims=True) + EPS) * g


def _sigmoid(x):
    return 1.0 / (1.0 + jnp.exp(-x))


def _rot_half(x, half):
    n = x.shape[-1]
    lane = lax.broadcasted_iota(jnp.int32, x.shape, x.ndim - 1)
    fwd = pltpu.roll(x, n - half, axis=x.ndim - 1)
    bwd = pltpu.roll(x, half, axis=x.ndim - 1)
    return jnp.where((lane & (2 * half - 1)) < half, fwd, bwd)


def _layer_spec(arr, layer, lead=()):
    rest = arr.shape[1 + len(lead):]
    idx = (layer,) + tuple(lead) + (0,) * len(rest)
    return pl.BlockSpec((None,) * (1 + len(lead)) + tuple(rest), lambda *_: idx,
                        pipeline_mode=pl.Buffered(1))


def _const_spec(arr):
    nd = arr.ndim
    return pl.BlockSpec(arr.shape, lambda *_: (0,) * nd, pipeline_mode=pl.Buffered(1))


def _rope_table_kernel(pos_ref, inv_ref, tab_ref):
    pos = pos_ref[0]
    ang = pos * inv_ref[0:1, :]
    c, s = jnp.cos(ang), jnp.sin(ang)
    lane = lax.broadcasted_iota(jnp.int32, ang.shape, 1)
    n_r, n_m = RET_DK // 2, MLA_ROPE // 2

    def spread_r(t):
        t = jnp.where(lane < n_r, t, pltpu.roll(t, n_r, axis=1))
        return jnp.where(lane < 2 * n_r, t, pltpu.roll(t, 2 * n_r, axis=1))

    def spread_m(t, fill):
        lo = pltpu.roll(t, MLA_NOPE - n_r, axis=1)
        hi = pltpu.roll(t, MLA_NOPE - n_r + n_m, axis=1)
        return jnp.where(lane < MLA_NOPE, fill,
                         jnp.where(lane < MLA_NOPE + n_m, lo,
                                   jnp.where(lane < MLA_NOPE + MLA_ROPE, hi, fill)))

    tab_ref[0, :, 0:128] = spread_r(c)
    tab_ref[0, :, 128:256] = spread_r(s) * inv_ref[2:3, :]
    tab_ref[0, :, 256:384] = spread_m(c, 1.0)
    tab_ref[0, :, 384:512] = spread_m(s, 0.0) * inv_ref[3:4, :]


def _rope_tables(positions):
    bsz, seq = positions.shape
    inv_r = ROPE_THETA ** (-jnp.arange(0, RET_DK, 2, dtype=F32) / RET_DK)
    inv_m = ROPE_THETA ** (-jnp.arange(0, MLA_ROPE, 2, dtype=F32) / MLA_ROPE)
    pad = MLA_HEAD_PAD - MLA_NOPE - MLA_ROPE
    lane_f = jnp.concatenate([inv_r, inv_m, jnp.zeros((LANES - inv_r.shape[0] - inv_m.shape[0],), F32)])
    sgn_r = np.tile(np.repeat(np.array([-1.0, 1.0], np.float32), RET_DK // 2), LANES // RET_DK)
    sgn_m = np.concatenate([np.zeros(MLA_NOPE, np.float32),
                            np.repeat(np.array([-1.0, 1.0], np.float32), MLA_ROPE // 2),
                            np.zeros(pad, np.float32)])
    inv = jnp.zeros((8, LANES), F32).at[0].set(lane_f).at[2].set(sgn_r).at[3].set(sgn_m)
    pos_b = jnp.broadcast_to(positions.astype(F32)[:, :, None], (bsz, seq, LANES))
    ts = min(TS_TABLE, seq)
    return pl.pallas_call(
        _rope_table_kernel,
        out_shape=jax.ShapeDtypeStruct((bsz, seq, 4 * LANES), F32),
        grid=(bsz, seq // ts),
        in_specs=[pl.BlockSpec((1, ts, LANES), lambda b, s: (b, s, 0)),
                  pl.BlockSpec((8, LANES), lambda b, s: (0, 0))],
        out_specs=pl.BlockSpec((1, ts, 4 * LANES), lambda b, s: (b, s, 0)),
        compiler_params=pltpu.CompilerParams(dimension_semantics=("arbitrary", "arbitrary")),
        name="rope_tables",
    )(pos_b, inv)


def _mix_front_kernel(x_ref, tab_ref, nrm_ref, wmain_ref, wm_ref, wg_ref, bg_ref,
                      cw_ref, lng_ref, lnb_ref, sgw_ref, sgb_ref, qn_ref, wq_ref,
                      kvn_ref, wkv_ref, wb_ref, dec_ref, xi_ref, zeta_ref,
                      mp_ref, gm_ref, q_ref, k_ref, v_ref,
                      conv_sc, state_sc, *, ts, chunk_decay):
    s_idx = pl.program_id(1)

    @pl.when(s_idx == 0)
    def _():
        conv_sc[0:CONV_HALO, :] = jnp.zeros((CONV_HALO, CONV_DIM), F32)
        state_sc[...] = jnp.zeros_like(state_sc)

    x = x_ref[0]
    hb = _rms(x, nrm_ref[...]).astype(BF16)
    tab = tab_ref[0]
    cr, sr = tab[:, 0:128], tab[:, 128:256]
    cm, sm = tab[:, 256:384], tab[:, 384:512]

    def gate(i):
        return _sigmoid(_dot(hb, wg_ref[:, i * D_MODEL:(i + 1) * D_MODEL]) + bg_ref[i:i + 1, :])

    ps = _dot(hb, wmain_ref[:, 3072:4096])
    u = jax.nn.gelu(ps[:, 0:512])
    sv = jax.nn.gelu(ps[:, 512:1024])
    mu = jnp.mean(sv, axis=-1, keepdims=True)
    svc = sv - mu
    vln = svc * lax.rsqrt(jnp.mean(svc * svc, axis=-1, keepdims=True) + EPS) * lng_ref[...] + lnb_ref[...]
    vlnb = vln.astype(BF16)
    row = lax.broadcasted_iota(jnp.int32, (CHUNK, CHUNK), 0)
    col = lax.broadcasted_iota(jnp.int32, (CHUNK, CHUNK), 1)
    wtril = [jnp.where(row >= col, sgw_ref[g], 0.0).astype(BF16) for g in range(SG_GROUPS)]
    s_chunks = []
    for c in range(ts // CHUNK):
        rows = slice(c * CHUNK, (c + 1) * CHUNK)
        s_chunks.append(jnp.concatenate(
            [_dot(wtril[g], vlnb[rows, g * CHUNK:(g + 1) * CHUNK]) + sgb_ref[g]
             for g in range(SG_GROUPS)], axis=1))
    y_s = (u * jnp.concatenate(s_chunks, axis=0)).astype(BF16)

    pm = _dot(hb, wm_ref[...])
    cqn = _rms(pm[:, 0:MLA_Q_RANK], qn_ref[...]).astype(BF16)
    ckvn = _rms(pm[:, MLA_Q_RANK:MLA_Q_RANK + MLA_KV_RANK], kvn_ref[...]).astype(BF16)
    cm8 = jnp.concatenate([cm] * MLA_HEADS, axis=1)
    sm8 = jnp.concatenate([sm] * MLA_HEADS, axis=1)
    qa = _dot(cqn, wq_ref[...])
    qf = (qa * cm8 + _rot_half(qa, MLA_ROPE // 2) * sm8) * QK_SCALE_LOG2E
    q_ref[0] = qf.astype(BF16)
    kp = pm[:, MLA_Q_RANK + MLA_KV_RANK:W_MLA_COLS]
    kpe = kp * cm + _rot_half(kp, MLA_ROPE // 2) * sm
    kv = _dot(ckvn, wkv_ref[...])
    lane = lax.broadcasted_iota(jnp.int32, kv.shape, 1) & (MLA_HEAD_PAD - 1)
    k_ref[0] = jnp.where(lane < MLA_NOPE, kv, jnp.concatenate([kpe] * MLA_HEADS, axis=1)).astype(BF16)
    ones_col = (lane == MLA_V).astype(F32)
    v_up = pltpu.roll(kv, kv.shape[1] - MLA_NOPE, axis=1)
    v_ref[0] = jnp.where(lane < MLA_V, v_up, ones_col).astype(BF16)

    pr = _dot(hb, wmain_ref[:, 1536:3072])
    pa = _dot(hb, wmain_ref[:, 0:1536])
    a_b, a_p = pa[:, 0:512], pa[:, 512:1024] * pa[:, 1024:1536]
    cr2 = jnp.concatenate([cr, cr], axis=1)
    sr2 = jnp.concatenate([sr, sr], axis=1)
    rq, rk = pr[:, 0:256], pr[:, 256:512]
    qr = rq * cr2 + _rot_half(rq, RET_DK // 2) * sr2
    kr = (rk * cr2 + _rot_half(rk, RET_DK // 2) * sr2) * (RET_DK ** -0.5)
    rv = pr[:, 512:1024].astype(BF16)
    rg = pr[:, 1024:1536]
    y_r_chunks = []
    y_a_chunks = []
    gates = []
    n_chunks = ts // CHUNK
    for c in range(n_chunks):
        rows = slice(c * CHUNK, (c + 1) * CHUNK)
        qc, kc = qr[rows], kr[rows]
        qcb, kcb = qc.astype(BF16), kc.astype(BF16)
        qx = (qc * xi_ref[...]).astype(BF16)
        kz = kc * zeta_ref[...]
        heads = []
        for h in range(RET_HEADS):
            hs = slice(h * RET_DK, (h + 1) * RET_DK)
            vs = slice(h * RET_DV, (h + 1) * RET_DV)
            vh = rv[rows, vs]
            scores = _dot_nt(qcb[:, hs], kcb[:, hs]) * dec_ref[h]
            state = state_sc[h]
            o = _dot(scores.astype(BF16), vh) + _dot(qx[:, hs], state.astype(BF16))
            mu = jnp.mean(o, axis=-1, keepdims=True)
            oc = o - mu
            on = oc * lax.rsqrt(jnp.mean(oc * oc, axis=-1, keepdims=True) + EPS)
            g = rg[rows, vs]
            heads.append(g * _sigmoid(g) * on)
            kzt = jnp.transpose(kz[:, hs]).astype(BF16)
            state_sc[h] = chunk_decay[h] * state + _dot(kzt, vh)
        y_r_chunks.append(jnp.concatenate(heads, axis=1))
        for i in range(len(gates), (c + 1) * N_BRANCH // n_chunks):
            gates.append(gate(GATE_ORDER[i]))
        p = a_p[rows]
        r0 = CONV_HALO + c * CHUNK
        conv_sc[r0:r0 + CHUNK, :] = p
        p1 = conv_sc[r0 - 1:r0 - 1 + CHUNK, :]
        p2 = conv_sc[r0 - 2:r0 - 2 + CHUNK, :]
        y_a_chunks.append(
            (a_b[rows] * (cw_ref[0:1, :] * p2 + cw_ref[1:2, :] * p1 + cw_ref[2:3, :] * p)).astype(BF16))
    conv_sc[0:CONV_HALO, :] = a_p[ts - CONV_HALO:ts, :]
    y_r = jnp.concatenate(y_r_chunks, axis=0).astype(BF16)
    y_a = jnp.concatenate(y_a_chunks, axis=0)
    g_s, g_m, g_a, g_r = gates
    gm_ref[0] = g_m.astype(BF16)
    mp = g_s * _dot(y_s, wb_ref[2]) + g_a * _dot(y_a, wb_ref[0]) + g_r * _dot(y_r, wb_ref[1])
    mp_ref[0] = mp.astype(BF16)


def _retention_tables():
    log_gamma = np.log1p(-np.exp2(-5.0 - np.arange(RET_HEADS, dtype=np.float64)))
    idx = np.arange(CHUNK, dtype=np.float64)
    diff = idx[:, None] - idx[None, :]
    decay = np.where(diff >= 0, np.exp(np.maximum(diff, 0.0)[None] * log_gamma[:, None, None]), 0.0)
    zeta = np.exp((CHUNK - 1 - idx)[:, None] * log_gamma[None, :])
    xi = np.exp((idx + 1.0)[:, None] * log_gamma[None, :])
    zeta_full = np.repeat(zeta, RET_DK, axis=1)
    xi_full = np.repeat(xi, RET_DK, axis=1)
    chunk_decay = tuple(float(v) for v in np.exp(CHUNK * log_gamma))
    return (jnp.asarray(decay, F32), jnp.asarray(xi_full, F32), jnp.asarray(zeta_full, F32), chunk_decay)


def _prep_weights(w_in, mla_w_uq, mla_w_ukv, w_branch, w_out, w_ffn_in, w_ffn_out):
    depth = w_in.shape[0]
    o = _OFF
    wmain = w_in[:, :, :W_MAIN_COLS].astype(BF16)
    pad_l = jnp.zeros((depth, D_MODEL, MLA_NOPE), F32)
    pad_r = jnp.zeros((depth, D_MODEL, MLA_HEAD_PAD - MLA_NOPE - MLA_ROPE), F32)
    wm = jnp.concatenate([w_in[:, :, o[9]:o[11]], pad_l, w_in[:, :, o[11]:o[12]], pad_r], axis=-1).astype(BF16)
    wg = w_in[:, :, o[12]:o[13]].astype(BF16)
    dq = MLA_NOPE + MLA_ROPE
    uq = mla_w_uq.reshape(depth, MLA_Q_RANK, MLA_HEADS, dq)
    zq = jnp.zeros((depth, MLA_Q_RANK, MLA_HEADS, MLA_HEAD_PAD - dq), F32)
    wq = jnp.concatenate([uq, zq], axis=-1).reshape(depth, MLA_Q_RANK, MLA_HEADS * MLA_HEAD_PAD).astype(BF16)
    assert MLA_NOPE + MLA_V == MLA_HEAD_PAD
    return dict(wmain=wmain, wm=wm, wg=wg, wq=wq, wkv=mla_w_ukv.astype(BF16), wb=w_branch.astype(BF16),
                wo=w_out.astype(BF16), wfi=w_ffn_in.astype(BF16), wfo=w_ffn_out.astype(BF16))


def _mix_front(layer, x, tab, w, norm_mix, b_gate, conv_w, sg_ln_g, sg_ln_b, sg_ws, sgb,
               mla_q_norm, mla_kv_norm, ret_tabs):
    bsz, seq, _ = x.shape
    ts = min(TS_FRONT, seq)
    decay, xi_full, zeta_full, chunk_decay = ret_tabs
    stacked = [norm_mix, w["wmain"], w["wm"], w["wg"], b_gate, conv_w, sg_ln_g, sg_ln_b, sg_ws, sgb,
               mla_q_norm, w["wq"], mla_kv_norm, w["wkv"], w["wb"]]
    consts = [decay, xi_full, zeta_full]
    tile = lambda wd: pl.BlockSpec((1, ts, wd), lambda b, s: (b, s, 0))
    qk_w = MLA_HEADS * MLA_HEAD_PAD
    out_shape = [jax.ShapeDtypeStruct((bsz, seq, wd), BF16)
                 for wd in (D_MODEL, D_MODEL, qk_w, qk_w, qk_w)]
    return pl.pallas_call(
        functools.partial(_mix_front_kernel, ts=ts, chunk_decay=chunk_decay),
        out_shape=out_shape,
        grid=(bsz, seq // ts),
        in_specs=[tile(D_MODEL), tile(4 * LANES)] + [_layer_spec(a, layer) for a in stacked]
                 + [_const_spec(c) for c in consts],
        out_specs=[tile(s.shape[-1]) for s in out_shape],
        scratch_shapes=[pltpu.VMEM((CONV_HALO + ts, CONV_DIM), F32),
                        pltpu.VMEM((RET_HEADS, RET_DK, RET_DV), F32)],
        compiler_params=pltpu.CompilerParams(dimension_semantics=("arbitrary", "arbitrary"),
                                             vmem_limit_bytes=VMEM_LIMIT),
        name="mix_front",
    )(x, tab, *stacked, *consts)


def _attn_kernel(q_ref, k_ref, v_ref, o_ref, s_sc, mx_sc, acc_sc, *, tq):
    qi = pl.program_id(1)
    half = tq // 2
    n_pairs = qi // 2
    heads = range(MLA_HEADS)
    hp = lambda h: slice(h * MLA_HEAD_PAD, (h + 1) * MLA_HEAD_PAD)

    def lane_max(s):
        parts = [s[:, i * half:(i + 1) * half] for i in range(s.shape[1] // half)]
        while len(parts) > 1:
            parts = [jnp.maximum(a, b) for a, b in zip(parts[0::2], parts[1::2])]
        return parts[0]

    row = lax.broadcasted_iota(jnp.int32, (tq, tq), 0)
    col = lax.broadcasted_iota(jnp.int32, (tq, tq), 1)
    kd = pl.multiple_of(qi * tq, tq)
    for h in heads:
        s = _dot_nt(q_ref[0, :, hp(h)], k_ref[0, pl.ds(kd, tq), hp(h)])
        s = jnp.where(col <= row, s, -1e30)
        s_sc[qi, h] = s
        mx_sc[h] = lane_max(s)

    def scores_pair(i, carry):
        ks = pl.multiple_of(i * 2 * tq, 2 * tq)
        for h in heads:
            s = _dot_nt(q_ref[0, :, hp(h)], k_ref[0, pl.ds(ks, 2 * tq), hp(h)])
            s_sc[2 * i, h] = s[:, :tq]
            s_sc[2 * i + 1, h] = s[:, tq:]
            mx_sc[h] = jnp.maximum(mx_sc[h], lane_max(s))
        return carry

    lax.fori_loop(0, n_pairs, scores_pair, 0)

    @pl.when(qi % 2 == 1)
    def _():
        ks = pl.multiple_of((qi - 1) * tq, tq)
        for h in heads:
            s = _dot_nt(q_ref[0, :, hp(h)], k_ref[0, pl.ds(ks, tq), hp(h)])
            s_sc[qi - 1, h] = s
            mx_sc[h] = jnp.maximum(mx_sc[h], lane_max(s))

    for h in heads:
        mx_sc[h] = jnp.broadcast_to(jnp.max(mx_sc[h], axis=-1, keepdims=True), (tq, half))

    def probs(blocks, h):
        m = mx_sc[h]
        m2 = jnp.concatenate([m, m], axis=1)
        return jnp.concatenate([jnp.exp2(s_sc[j, h] - m2) for j in blocks], axis=1).astype(BF16)

    for h in heads:
        acc_sc[h] = _dot(probs([qi], h), v_ref[0, pl.ds(kd, tq), hp(h)])

    def pv_pair(i, carry):
        ks = pl.multiple_of(i * 2 * tq, 2 * tq)
        for h in heads:
            acc_sc[h] += _dot(probs([2 * i, 2 * i + 1], h), v_ref[0, pl.ds(ks, 2 * tq), hp(h)])
        return carry

    lax.fori_loop(0, n_pairs, pv_pair, 0)

    @pl.when(qi % 2 == 1)
    def _():
        ks = pl.multiple_of((qi - 1) * tq, tq)
        for h in heads:
            acc_sc[h] += _dot(probs([qi - 1], h), v_ref[0, pl.ds(ks, tq), hp(h)])

    for h2 in range(MLA_HEADS // 2):
        outs = []
        for h in (2 * h2, 2 * h2 + 1):
            acc = acc_sc[h]
            outs.append(acc[:, :MLA_V] * (1.0 / acc[:, MLA_V:MLA_V + 1]))
        o_ref[0, :, h2 * 2 * MLA_V:(h2 + 1) * 2 * MLA_V] = jnp.concatenate(outs, axis=1).astype(BF16)


def _attention(q, k, v):
    bsz, seq, qk_w = q.shape
    tq = min(TQ_ATTN, seq)
    o_w = MLA_HEADS * MLA_V
    return pl.pallas_call(
        functools.partial(_attn_kernel, tq=tq),
        out_shape=jax.ShapeDtypeStruct((bsz, seq, o_w), BF16),
        grid=(bsz, seq // tq),
        in_specs=[pl.BlockSpec((1, tq, qk_w), lambda b, i: (b, i, 0)),
                  pl.BlockSpec((1, seq, qk_w), lambda b, i: (b, 0, 0)),
                  pl.BlockSpec((1, seq, qk_w), lambda b, i: (b, 0, 0))],
        out_specs=pl.BlockSpec((1, tq, o_w), lambda b, i: (b, i, 0)),
        scratch_shapes=[pltpu.VMEM((seq // tq, MLA_HEADS, tq, tq), F32),
                        pltpu.VMEM((MLA_HEADS, tq, tq // 2), F32),
                        pltpu.VMEM((MLA_HEADS, tq, MLA_HEAD_PAD), F32)],
        compiler_params=pltpu.CompilerParams(dimension_semantics=("arbitrary", "arbitrary"),
                                             vmem_limit_bytes=VMEM_LIMIT),
        name="mla_attention",
    )(q, k, v)


def _mix_back_kernel(x_ref, mp_ref, gm_ref, ym_ref, wbm_ref, wo_ref, nf_ref, wfi_ref, wfo_ref,
                     fn_ref, o_ref, *, final):
    merged = mp_ref[...].astype(F32) + gm_ref[...].astype(F32) * _dot(ym_ref[...], wbm_ref[...])
    x1 = x_ref[...] + _dot(merged.astype(BF16), wo_ref[...])
    hn = _rms(x1, nf_ref[...]).astype(BF16)
    acc = jnp.zeros_like(x1)
    for c in range(D_FF // FFN_COLS):
        g = _dot(hn, wfi_ref[:, c * FFN_COLS:(c + 1) * FFN_COLS])
        u = _dot(hn, wfi_ref[:, D_FF + c * FFN_COLS:D_FF + (c + 1) * FFN_COLS])
        act = (g * _sigmoid(g) * u).astype(BF16)
        acc = acc + _dot(act, wfo_ref[c * FFN_COLS:(c + 1) * FFN_COLS, :])
    x2 = x1 + acc
    if final:
        x2 = _rms(x2, fn_ref[...])
    o_ref[...] = x2


def _mix_back(layer, x, mp, gm, ym, w, norm_ffn, final_norm, final):
    t = x.shape[0]
    tm = min(TM_BACK, t)
    tile = lambda wd: pl.BlockSpec((tm, wd), lambda i: (i, 0))
    return pl.pallas_call(
        functools.partial(_mix_back_kernel, final=final),
        out_shape=jax.ShapeDtypeStruct((t, D_MODEL), F32),
        grid=(t // tm,),
        in_specs=[tile(D_MODEL), tile(D_MODEL), tile(D_MODEL), tile(BRANCH_W),
                  _layer_spec(w["wb"], layer, lead=(N_BRANCH - 1,)), _layer_spec(w["wo"], layer),
                  _layer_spec(norm_ffn, layer), _layer_spec(w["wfi"], layer), _layer_spec(w["wfo"], layer),
                  _const_spec(final_norm)],
        out_specs=tile(D_MODEL),
        compiler_params=pltpu.CompilerParams(dimension_semantics=("arbitrary",),
                                             vmem_limit_bytes=VMEM_LIMIT),
        name="mix_back_ffn",
    )(x, mp, gm, ym, w["wb"], w["wo"], norm_ffn, w["wfi"], w["wfo"], final_norm)


def kernel(x, positions, norm_mix, w_in, b_gate, conv_w, sg_ln_g, sg_ln_b, sg_ws, sg_b, mla_q_norm,
           mla_w_uq, mla_kv_norm, mla_w_ukv, w_branch, w_out, norm_ffn, w_ffn_in, w_ffn_out, final_norm):
    bsz, seq, d = x.shape
    depth = w_in.shape[0]
    tab = _rope_tables(positions)
    w = _prep_weights(w_in, mla_w_uq, mla_w_ukv, w_branch, w_out, w_ffn_in, w_ffn_out)
    ret_tabs = _retention_tables()
    row = lambda a: a.reshape(depth, 1, a.shape[-1])
    sgb = jnp.broadcast_to(sg_b[:, :, :, None], (depth, SG_GROUPS, CHUNK, CHUNK))
    norm_mix, sg_ln_g, sg_ln_b = row(norm_mix), row(sg_ln_g), row(sg_ln_b)
    mla_q_norm, mla_kv_norm, norm_ffn = row(mla_q_norm), row(mla_kv_norm), row(norm_ffn)
    final_norm = final_norm.reshape(1, d)
    flat = lambda a: a.reshape(bsz * seq, a.shape[-1])
    for l in range(depth):
        mp, gm, q, k, v = _mix_front(l, x, tab, w, norm_mix, b_gate, conv_w, sg_ln_g, sg_ln_b, sg_ws, sgb,
                                     mla_q_norm, mla_kv_norm, ret_tabs)
        ym = _attention(q, k, v)
        x = _mix_back(l, flat(x), flat(mp), flat(gm), flat(ym), w, norm_ffn, final_norm,
                      final=(l == depth - 1)).reshape(bsz, seq, d)
    return x
```

```python
import functools

import numpy as np
import jax
import jax.numpy as jnp
from jax import lax
from jax.experimental import pallas as pl
from jax.experimental.pallas import tpu as pltpu

F32 = jnp.float32
BF16 = jnp.bfloat16

D_MODEL = 1024
N_BRANCH = 4
BRANCH_W = 512
EPS = 1e-6
ROPE_THETA = 10000.0
CHUNK = 128
CONV_DIM = 512
CONV_W = 3
RET_HEADS = 4
RET_DK = 64
RET_DV = 128
SG_DIM = 512
SG_GROUPS = 4
MLA_HEADS = 8
MLA_NOPE = 64
MLA_ROPE = 32
MLA_V = 64
MLA_Q_RANK = 384
MLA_KV_RANK = 256
D_FF = 2816

LANES = 128
MLA_HEAD_PAD = 128
CONV_HALO = 8
FFN_COLS = 256
VMEM_LIMIT = 60 * 1024 * 1024
QK_SCALE_LOG2E = float((MLA_NOPE + MLA_ROPE) ** -0.5 * np.log2(np.e))

TS_FRONT = 512
TQ_ATTN = 256
TM_BACK = 512
TS_TABLE = 512

_OFF = [int(v) for v in np.cumsum([0, 512, 512, 512, 256, 256, 512, 512, 512, 512, 384, 256, 32, 4096])]
W_MAIN_COLS = _OFF[9]
W_MLA_COLS = MLA_Q_RANK + MLA_KV_RANK + MLA_HEAD_PAD
GATE_ORDER = (2, 3, 0, 1)


def _dot(a, b):
    return jnp.dot(a, b, preferred_element_type=F32)


def _dot_nt(a, b):
    return lax.dot_general(a, b, (((1,), (1,)), ((), ())), preferred_element_type=F32)


def _rms(x, g):
    return x * lax.rsqrt(jnp.mean(x * x, axis=-1, keepdims=True) + EPS) * g


def _sigmoid(x):
    return 1.0 / (1.0 + jnp.exp(-x))


def _rot_half(x, half):
    n = x.shape[-1]
    lane = lax.broadcasted_iota(jnp.int32, x.shape, x.ndim - 1)
    fwd = pltpu.roll(x, n - half, axis=x.ndim - 1)
    bwd = pltpu.roll(x, half, axis=x.ndim - 1)
    return jnp.where((lane & (2 * half - 1)) < half, fwd, bwd)


def _layer_spec(arr, layer, lead=()):
    rest = arr.shape[1 + len(lead):]
    idx = (layer,) + tuple(lead) + (0,) * len(rest)
    return pl.BlockSpec((None,) * (1 + len(lead)) + tuple(rest), lambda *_: idx,
                        pipeline_mode=pl.Buffered(1))


def _const_spec(arr):
    nd = arr.ndim
    return pl.BlockSpec(arr.shape, lambda *_: (0,) * nd, pipeline_mode=pl.Buffered(1))


def _rope_table_kernel(pos_ref, inv_ref, tab_ref):
    pos = pos_ref[0]
    ang = pos * inv_ref[0:1, :]
    c, s = jnp.cos(ang), jnp.sin(ang)
    lane = lax.broadcasted_iota(jnp.int32, ang.shape, 1)
    n_r, n_m = RET_DK // 2, MLA_ROPE // 2

    def spread_r(t):
        t = jnp.where(lane < n_r, t, pltpu.roll(t, n_r, axis=1))
        return jnp.where(lane < 2 * n_r, t, pltpu.roll(t, 2 * n_r, axis=1))

    def spread_m(t, fill):
        lo = pltpu.roll(t, MLA_NOPE - n_r, axis=1)
        hi = pltpu.roll(t, MLA_NOPE - n_r + n_m, axis=1)
        return jnp.where(lane < MLA_NOPE, fill,
                         jnp.where(lane < MLA_NOPE + n_m, lo,
                                   jnp.where(lane < MLA_NOPE + MLA_ROPE, hi, fill)))

    tab_ref[0, :, 0:128] = spread_r(c)
    tab_ref[0, :, 128:256] = spread_r(s) * inv_ref[2:3, :]
    tab_ref[0, :, 256:384] = spread_m(c, 1.0)
    tab_ref[0, :, 384:512] = spread_m(s, 0.0) * inv_ref[3:4, :]


def _rope_tables(positions):
    bsz, seq = positions.shape
    inv_r = ROPE_THETA ** (-jnp.arange(0, RET_DK, 2, dtype=F32) / RET_DK)
    inv_m = ROPE_THETA ** (-jnp.arange(0, MLA_ROPE, 2, dtype=F32) / MLA_ROPE)
    pad = MLA_HEAD_PAD - MLA_NOPE - MLA_ROPE
    lane_f = jnp.concatenate([inv_r, inv_m, jnp.zeros((LANES - inv_r.shape[0] - inv_m.shape[0],), F32)])
    sgn_r = np.tile(np.repeat(np.array([-1.0, 1.0], np.float32), RET_DK // 2), LANES // RET_DK)
    sgn_m = np.concatenate([np.zeros(MLA_NOPE, np.float32),
                            np.repeat(np.array([-1.0, 1.0], np.float32), MLA_ROPE // 2),
                            np.zeros(pad, np.float32)])
    inv = jnp.zeros((8, LANES), F32).at[0].set(lane_f).at[2].set(sgn_r).at[3].set(sgn_m)
    pos_b = jnp.broadcast_to(positions.astype(F32)[:, :, None], (bsz, seq, LANES))
    ts = min(TS_TABLE, seq)
    return pl.pallas_call(
        _rope_table_kernel,
        out_shape=jax.ShapeDtypeStruct((bsz, seq, 4 * LANES), F32),
        grid=(bsz, seq // ts),
        in_specs=[pl.BlockSpec((1, ts, LANES), lambda b, s: (b, s, 0)),
                  pl.BlockSpec((8, LANES), lambda b, s: (0, 0))],
        out_specs=pl.BlockSpec((1, ts, 4 * LANES), lambda b, s: (b, s, 0)),
        compiler_params=pltpu.CompilerParams(dimension_semantics=("arbitrary", "arbitrary")),
        name="rope_tables",
    )(pos_b, inv)


def _mix_front_kernel(x_ref, tab_ref, nrm_ref, wmain_ref, wm_ref, wg_ref, bg_ref,
                      cw_ref, lng_ref, lnb_ref, sgw_ref, sgb_ref, qn_ref, wq_ref,
                      kvn_ref, wkv_ref, wb_ref, dec_ref, xi_ref, zeta_ref,
                      mp_ref, gm_ref, q_ref, k_ref, v_ref,
                      conv_sc, state_sc, *, ts, chunk_decay):
    s_idx = pl.program_id(1)

    @pl.when(s_idx == 0)
    def _():
        conv_sc[0:CONV_HALO, :] = jnp.zeros((CONV_HALO, CONV_DIM), F32)
        state_sc[...] = jnp.zeros_like(state_sc)

    x = x_ref[0]
    hb = _rms(x, nrm_ref[...]).astype(BF16)
    tab = tab_ref[0]
    cr, sr = tab[:, 0:128], tab[:, 128:256]
    cm, sm = tab[:, 256:384], tab[:, 384:512]

    n_chunks = ts // CHUNK
    half_d = D_MODEL // 2
    rows_of = lambda c: slice(c * CHUNK, (c + 1) * CHUNK)

    def proj(ref, lo):
        return lambda: _dot(hb, ref[:, lo:lo + FFN_COLS])

    def gate_piece(i, lo):
        return lambda: _sigmoid(_dot(hb, wg_ref[:, i * D_MODEL + lo:i * D_MODEL + lo + FFN_COLS])
                                + bg_ref[i:i + 1, lo:lo + FFN_COLS])

    def gate_m_piece(lo):
        def fn():
            gm_ref[0, :, lo:lo + FFN_COLS] = gate_piece(N_BRANCH - 1, lo)().astype(BF16)
        return fn

    def pieces(name, make, lo, hi):
        return [(name, make(c)) for c in range(lo, hi, FFN_COLS)]

    queue = (pieces("rqk", lambda c: proj(wmain_ref, c), 1536, 2048)
             + pieces("rv", lambda c: proj(wmain_ref, c), 2048, 2560)
             + pieces("rg", lambda c: proj(wmain_ref, c), 2560, 3072)
             + pieces("ab", lambda c: proj(wmain_ref, c), 0, 512)
             + pieces("ac", lambda c: proj(wmain_ref, c), 512, 1024)
             + pieces("ax", lambda c: proj(wmain_ref, c), 1024, 1536)
             + pieces("gs", lambda c: gate_piece(2, c), 0, D_MODEL)
             + pieces("gm", gate_m_piece, 0, D_MODEL)
             + pieces("ga", lambda c: gate_piece(0, c), 0, D_MODEL)
             + pieces("gr", lambda c: gate_piece(1, c), 0, D_MODEL))
    done = {}

    def emit(n=1):
        for _ in range(min(n, len(queue))):
            name, fn = queue.pop(0)
            done.setdefault(name, []).append(fn())

    def need(name):
        while any(nm == name for nm, _ in queue):
            emit()
        return jnp.concatenate(done[name], axis=1)

    ps = _dot(hb, wmain_ref[:, 3072:4096])
    pm = _dot(hb, wm_ref[...])

    row = lax.broadcasted_iota(jnp.int32, (CHUNK, CHUNK), 0)
    col = lax.broadcasted_iota(jnp.int32, (CHUNK, CHUNK), 1)
    wtril = [jnp.where(row >= col, sgw_ref[g], 0.0).astype(BF16) for g in range(SG_GROUPS)]
    y_s_chunks = []
    for c in range(n_chunks):
        rows = rows_of(c)
        u = jax.nn.gelu(ps[rows, 0:512])
        sv = jax.nn.gelu(ps[rows, 512:1024])
        svc = sv - jnp.mean(sv, axis=-1, keepdims=True)
        vln = svc * lax.rsqrt(jnp.mean(svc * svc, axis=-1, keepdims=True) + EPS) * lng_ref[...] + lnb_ref[...]
        vlnb = vln.astype(BF16)
        s = jnp.concatenate([_dot(wtril[g], vlnb[:, g * CHUNK:(g + 1) * CHUNK]) + sgb_ref[g]
                             for g in range(SG_GROUPS)], axis=1)
        y_s_chunks.append((u * s).astype(BF16))
        emit(2)
    y_s = jnp.concatenate(y_s_chunks, axis=0)

    cqn = _rms(pm[:, 0:MLA_Q_RANK], qn_ref[...]).astype(BF16)
    ckvn = _rms(pm[:, MLA_Q_RANK:MLA_Q_RANK + MLA_KV_RANK], kvn_ref[...]).astype(BF16)
    qa = _dot(cqn, wq_ref[...])
    kv = _dot(ckvn, wkv_ref[...])
    lane = lax.broadcasted_iota(jnp.int32, (CHUNK, MLA_HEADS * MLA_HEAD_PAD), 1) & (MLA_HEAD_PAD - 1)
    ones_col = (lane == MLA_V).astype(F32)
    for c in range(n_chunks):
        rows = rows_of(c)
        cm_c, sm_c = cm[rows], sm[rows]
        cm8 = jnp.concatenate([cm_c] * MLA_HEADS, axis=1)
        sm8 = jnp.concatenate([sm_c] * MLA_HEADS, axis=1)
        qa_c = qa[rows]
        q_ref[0, rows, :] = ((qa_c * cm8 + _rot_half(qa_c, MLA_ROPE // 2) * sm8) * QK_SCALE_LOG2E).astype(BF16)
        kp = pm[rows, MLA_Q_RANK + MLA_KV_RANK:W_MLA_COLS]
        kpe = kp * cm_c + _rot_half(kp, MLA_ROPE // 2) * sm_c
        kv_c = kv[rows]
        k_ref[0, rows, :] = jnp.where(lane < MLA_NOPE, kv_c,
                                      jnp.concatenate([kpe] * MLA_HEADS, axis=1)).astype(BF16)
        v_up = pltpu.roll(kv_c, kv_c.shape[1] - MLA_NOPE, axis=1)
        v_ref[0, rows, :] = jnp.where(lane < MLA_V, v_up, ones_col).astype(BF16)
        emit(2)

    rqk, rv, rg = need("rqk"), need("rv").astype(BF16), need("rg")
    a_b, a_p = need("ab"), need("ac") * need("ax")
    y_r_chunks = []
    y_a_chunks = []
    for c in range(n_chunks):
        rows = rows_of(c)
        cr2 = jnp.concatenate([cr[rows], cr[rows]], axis=1)
        sr2 = jnp.concatenate([sr[rows], sr[rows]], axis=1)
        rq, rk = rqk[rows, 0:256], rqk[rows, 256:512]
        qc = rq * cr2 + _rot_half(rq, RET_DK // 2) * sr2
        kc = (rk * cr2 + _rot_half(rk, RET_DK // 2) * sr2) * (RET_DK ** -0.5)
        qcb, kcb = qc.astype(BF16), kc.astype(BF16)
        qx = (qc * xi_ref[...]).astype(BF16)
        kz = kc * zeta_ref[...]
        heads = []
        for h in range(RET_HEADS):
            hs = slice(h * RET_DK, (h + 1) * RET_DK)
            vs = slice(h * RET_DV, (h + 1) * RET_DV)
            vh = rv[rows, vs]
            scores = _dot_nt(qcb[:, hs], kcb[:, hs]) * dec_ref[h]
            state = state_sc[h]
            o = _dot(scores.astype(BF16), vh) + _dot(qx[:, hs], state.astype(BF16))
            oc = o - jnp.mean(o, axis=-1, keepdims=True)
            on = oc * lax.rsqrt(jnp.mean(oc * oc, axis=-1, keepdims=True) + EPS)
            g = rg[rows, vs]
            heads.append((g * _sigmoid(g) * on).astype(BF16))
            kzt = jnp.transpose(kz[:, hs]).astype(BF16)
            state_sc[h] = chunk_decay[h] * state + _dot(kzt, vh)
            emit()
        y_r_chunks.append(jnp.concatenate(heads, axis=1))
        p = a_p[rows]
        r0 = CONV_HALO + c * CHUNK
        conv_sc[r0:r0 + CHUNK, :] = p
        p1 = conv_sc[r0 - 1:r0 - 1 + CHUNK, :]
        p2 = conv_sc[r0 - 2:r0 - 2 + CHUNK, :]
        y_a_chunks.append(
            (a_b[rows] * (cw_ref[0:1, :] * p2 + cw_ref[1:2, :] * p1 + cw_ref[2:3, :] * p)).astype(BF16))
    conv_sc[0:CONV_HALO, :] = a_p[ts - CONV_HALO:ts, :]
    y_r = jnp.concatenate(y_r_chunks, axis=0)
    y_a = jnp.concatenate(y_a_chunks, axis=0)
    emit(len(queue))

    g_s, g_a, g_r = need("gs"), need("ga"), need("gr")
    for j in range(2):
        cols = slice(j * half_d, (j + 1) * half_d)
        mp = (g_s[:, cols] * _dot(y_s, wb_ref[2, :, cols]) + g_a[:, cols] * _dot(y_a, wb_ref[0, :, cols])
              + g_r[:, cols] * _dot(y_r, wb_ref[1, :, cols]))
        mp_ref[0, :, cols] = mp.astype(BF16)


def _retention_tables():
    log_gamma = np.log1p(-np.exp2(-5.0 - np.arange(RET_HEADS, dtype=np.float64)))
    idx = np.arange(CHUNK, dtype=np.float64)
    diff = idx[:, None] - idx[None, :]
    decay = np.where(diff >= 0, np.exp(np.maximum(diff, 0.0)[None] * log_gamma[:, None, None]), 0.0)
    zeta = np.exp((CHUNK - 1 - idx)[:, None] * log_gamma[None, :])
    xi = np.exp((idx + 1.0)[:, None] * log_gamma[None, :])
    zeta_full = np.repeat(zeta, RET_DK, axis=1)
    xi_full = np.repeat(xi, RET_DK, axis=1)
    chunk_decay = tuple(float(v) for v in np.exp(CHUNK * log_gamma))
    return (jnp.asarray(decay, F32), jnp.asarray(xi_full, F32), jnp.asarray(zeta_full, F32), chunk_decay)


def _prep_weights(w_in, mla_w_uq, mla_w_ukv, w_branch, w_out, w_ffn_in, w_ffn_out):
    depth = w_in.shape[0]
    o = _OFF
    wmain = w_in[:, :, :W_MAIN_COLS].astype(BF16)
    pad_l = jnp.zeros((depth, D_MODEL, MLA_NOPE), F32)
    pad_r = jnp.zeros((depth, D_MODEL, MLA_HEAD_PAD - MLA_NOPE - MLA_ROPE), F32)
    wm = jnp.concatenate([w_in[:, :, o[9]:o[11]], pad_l, w_in[:, :, o[11]:o[12]], pad_r], axis=-1).astype(BF16)
    wg = w_in[:, :, o[12]:o[13]].astype(BF16)
    dq = MLA_NOPE + MLA_ROPE
    uq = mla_w_uq.reshape(depth, MLA_Q_RANK, MLA_HEADS, dq)
    zq = jnp.zeros((depth, MLA_Q_RANK, MLA_HEADS, MLA_HEAD_PAD - dq), F32)
    wq = jnp.concatenate([uq, zq], axis=-1).reshape(depth, MLA_Q_RANK, MLA_HEADS * MLA_HEAD_PAD).astype(BF16)
    assert MLA_NOPE + MLA_V == MLA_HEAD_PAD
    return dict(wmain=wmain, wm=wm, wg=wg, wq=wq, wkv=mla_w_ukv.astype(BF16), wb=w_branch.astype(BF16),
                wo=w_out.astype(BF16), wfi=w_ffn_in.astype(BF16), wfo=w_ffn_out.astype(BF16))


def _mix_front(layer, x, tab, w, norm_mix, b_gate, conv_w, sg_ln_g, sg_ln_b, sg_ws, sgb,
               mla_q_norm, mla_kv_norm, ret_tabs):
    bsz, seq, _ = x.shape
    ts = min(TS_FRONT, seq)
    decay, xi_full, zeta_full, chunk_decay = ret_tabs
    stacked = [norm_mix, w["wmain"], w["wm"], w["wg"], b_gate, conv_w, sg_ln_g, sg_ln_b, sg_ws, sgb,
               mla_q_norm, w["wq"], mla_kv_norm, w["wkv"], w["wb"]]
    consts = [decay, xi_full, zeta_full]
    tile = lambda wd: pl.BlockSpec((1, ts, wd), lambda b, s: (b, s, 0))
    qk_w = MLA_HEADS * MLA_HEAD_PAD
    out_shape = [jax.ShapeDtypeStruct((bsz, seq, wd), BF16)
                 for wd in (D_MODEL, D_MODEL, qk_w, qk_w, qk_w)]
    return pl.pallas_call(
        functools.partial(_mix_front_kernel, ts=ts, chunk_decay=chunk_decay),
        out_shape=out_shape,
        grid=(bsz, seq // ts),
        in_specs=[tile(D_MODEL), tile(4 * LANES)] + [_layer_spec(a, layer) for a in stacked]
                 + [_const_spec(c) for c in consts],
        out_specs=[tile(s.shape[-1]) for s in out_shape],
        scratch_shapes=[pltpu.VMEM((CONV_HALO + ts, CONV_DIM), F32),
                        pltpu.VMEM((RET_HEADS, RET_DK, RET_DV), F32)],
        compiler_params=pltpu.CompilerParams(dimension_semantics=("arbitrary", "arbitrary"),
                                             vmem_limit_bytes=VMEM_LIMIT),
        name="mix_front",
    )(x, tab, *stacked, *consts)


def _attn_kernel(q_ref, k_ref, v_ref, o_ref, s_sc, mx_sc, acc_sc, *, tq):
    qi = pl.program_id(1)
    half = tq // 2
    n_pairs = qi // 2
    heads = range(MLA_HEADS)
    hp = lambda h: slice(h * MLA_HEAD_PAD, (h + 1) * MLA_HEAD_PAD)

    def lane_max(s):
        parts = [s[:, i * half:(i + 1) * half] for i in range(s.shape[1] // half)]
        while len(parts) > 1:
            parts = [jnp.maximum(a, b) for a, b in zip(parts[0::2], parts[1::2])]
        return parts[0]

    row = lax.broadcasted_iota(jnp.int32, (tq, tq), 0)
    col = lax.broadcasted_iota(jnp.int32, (tq, tq), 1)
    kd = pl.multiple_of(qi * tq, tq)
    for h in heads:
        s = _dot_nt(q_ref[0, :, hp(h)], k_ref[0, pl.ds(kd, tq), hp(h)])
        s = jnp.where(col <= row, s, -1e30)
        s_sc[qi, h] = s
        mx_sc[h] = lane_max(s)

    def scores_pair(i, carry):
        ks = pl.multiple_of(i * 2 * tq, 2 * tq)
        for h in heads:
            s = _dot_nt(q_ref[0, :, hp(h)], k_ref[0, pl.ds(ks, 2 * tq), hp(h)])
            s_sc[2 * i, h] = s[:, :tq]
            s_sc[2 * i + 1, h] = s[:, tq:]
            mx_sc[h] = jnp.maximum(mx_sc[h], lane_max(s))
        return carry

    lax.fori_loop(0, n_pairs, scores_pair, 0)

    @pl.when(qi % 2 == 1)
    def _():
        ks = pl.multiple_of((qi - 1) * tq, tq)
        for h in heads:
            s = _dot_nt(q_ref[0, :, hp(h)], k_ref[0, pl.ds(ks, tq), hp(h)])
            s_sc[qi - 1, h] = s
            mx_sc[h] = jnp.maximum(mx_sc[h], lane_max(s))

    for h in heads:
        mx_sc[h] = jnp.broadcast_to(jnp.max(mx_sc[h], axis=-1, keepdims=True), (tq, half))

    def probs(blocks, h):
        m = mx_sc[h]
        m2 = jnp.concatenate([m, m], axis=1)
        return jnp.concatenate([jnp.exp2(s_sc[j, h] - m2) for j in blocks], axis=1).astype(BF16)

    for h in heads:
        acc_sc[h] = _dot(probs([qi], h), v_ref[0, pl.ds(kd, tq), hp(h)])

    def pv_pair(i, carry):
        ks = pl.multiple_of(i * 2 * tq, 2 * tq)
        for h in heads:
            acc_sc[h] += _dot(probs([2 * i, 2 * i + 1], h), v_ref[0, pl.ds(ks, 2 * tq), hp(h)])
        return carry

    lax.fori_loop(0, n_pairs, pv_pair, 0)

    @pl.when(qi % 2 == 1)
    def _():
        ks = pl.multiple_of((qi - 1) * tq, tq)
        for h in heads:
            acc_sc[h] += _dot(probs([qi - 1], h), v_ref[0, pl.ds(ks, tq), hp(h)])

    for h2 in range(MLA_HEADS // 2):
        outs = []
        for h in (2 * h2, 2 * h2 + 1):
            acc = acc_sc[h]
            outs.append(acc[:, :MLA_V] * (1.0 / acc[:, MLA_V:MLA_V + 1]))
        o_ref[0, :, h2 * 2 * MLA_V:(h2 + 1) * 2 * MLA_V] = jnp.concatenate(outs, axis=1).astype(BF16)


def _attention(q, k, v):
    bsz, seq, qk_w = q.shape
    tq = min(TQ_ATTN, seq)
    o_w = MLA_HEADS * MLA_V
    return pl.pallas_call(
        functools.partial(_attn_kernel, tq=tq),
        out_shape=jax.ShapeDtypeStruct((bsz, seq, o_w), BF16),
        grid=(bsz, seq // tq),
        in_specs=[pl.BlockSpec((1, tq, qk_w), lambda b, i: (b, i, 0)),
                  pl.BlockSpec((1, seq, qk_w), lambda b, i: (b, 0, 0)),
                  pl.BlockSpec((1, seq, qk_w), lambda b, i: (b, 0, 0))],
        out_specs=pl.BlockSpec((1, tq, o_w), lambda b, i: (b, i, 0)),
        scratch_shapes=[pltpu.VMEM((seq // tq, MLA_HEADS, tq, tq), F32),
                        pltpu.VMEM((MLA_HEADS, tq, tq // 2), F32),
                        pltpu.VMEM((MLA_HEADS, tq, MLA_HEAD_PAD), F32)],
        compiler_params=pltpu.CompilerParams(dimension_semantics=("arbitrary", "arbitrary"),
                                             vmem_limit_bytes=VMEM_LIMIT),
        name="mla_attention",
    )(q, k, v)


def _mix_back_kernel(x_ref, mp_ref, gm_ref, ym_ref, wbm_ref, wo_ref, nf_ref, wfi_ref, wfo_ref,
                     fn_ref, o_ref, *, final):
    merged = mp_ref[...].astype(F32) + gm_ref[...].astype(F32) * _dot(ym_ref[...], wbm_ref[...])
    x1 = x_ref[...] + _dot(merged.astype(BF16), wo_ref[...])
    hn = _rms(x1, nf_ref[...]).astype(BF16)
    acc = jnp.zeros_like(x1)
    for c in range(D_FF // FFN_COLS):
        g = _dot(hn, wfi_ref[:, c * FFN_COLS:(c + 1) * FFN_COLS])
        u = _dot(hn, wfi_ref[:, D_FF + c * FFN_COLS:D_FF + (c + 1) * FFN_COLS])
        act = (g * _sigmoid(g) * u).astype(BF16)
        acc = acc + _dot(act, wfo_ref[c * FFN_COLS:(c + 1) * FFN_COLS, :])
    x2 = x1 + acc
    if final:
        x2 = _rms(x2, fn_ref[...])
    o_ref[...] = x2


def _mix_back(layer, x, mp, gm, ym, w, norm_ffn, final_norm, final):
    t = x.shape[0]
    tm = min(TM_BACK, t)
    tile = lambda wd: pl.BlockSpec((tm, wd), lambda i: (i, 0))
    return pl.pallas_call(
        functools.partial(_mix_back_kernel, final=final),
        out_shape=jax.ShapeDtypeStruct((t, D_MODEL), F32),
        grid=(t // tm,),
        in_specs=[tile(D_MODEL), tile(D_MODEL), tile(D_MODEL), tile(BRANCH_W),
                  _layer_spec(w["wb"], layer, lead=(N_BRANCH - 1,)), _layer_spec(w["wo"], layer),
                  _layer_spec(norm_ffn, layer), _layer_spec(w["wfi"], layer), _layer_spec(w["wfo"], layer),
                  _const_spec(final_norm)],
        out_specs=tile(D_MODEL),
        compiler_params=pltpu.CompilerParams(dimension_semantics=("arbitrary",),
                                             vmem_limit_bytes=VMEM_LIMIT),
        name="mix_back_ffn",
    )(x, mp, gm, ym, w["wb"], w["wo"], norm_ffn, w["wfi"], w["wfo"], final_norm)


def kernel(x, positions, norm_mix, w_in, b_gate, conv_w, sg_ln_g, sg_ln_b, sg_ws, sg_b, mla_q_norm,
           mla_w_uq, mla_kv_norm, mla_w_ukv, w_branch, w_out, norm_ffn, w_ffn_in, w_ffn_out, final_norm):
    bsz, seq, d = x.shape
    depth = w_in.shape[0]
    tab = _rope_tables(positions)
    w = _prep_weights(w_in, mla_w_uq, mla_w_ukv, w_branch, w_out, w_ffn_in, w_ffn_out)
    ret_tabs = _retention_tables()
    row = lambda a: a.reshape(depth, 1, a.shape[-1])
    sgb = jnp.broadcast_to(sg_b[:, :, :, None], (depth, SG_GROUPS, CHUNK, CHUNK))
    norm_mix, sg_ln_g, sg_ln_b = row(norm_mix), row(sg_ln_g), row(sg_ln_b)
    mla_q_norm, mla_kv_norm, norm_ffn = row(mla_q_norm), row(mla_kv_norm), row(norm_ffn)
    final_norm = final_norm.reshape(1, d)
    flat = lambda a: a.reshape(bsz * seq, a.shape[-1])
    for l in range(depth):
        mp, gm, q, k, v = _mix_front(l, x, tab, w, norm_mix, b_gate, conv_w, sg_ln_g, sg_ln_b, sg_ws, sgb,
                                     mla_q_norm, mla_kv_norm, ret_tabs)
        ym = _attention(q, k, v)
        x = _mix_back(l, flat(x), flat(mp), flat(gm), flat(ym), w, norm_ffn, final_norm,
                      final=(l == depth - 1)).reshape(bsz, seq, d)
    return x
```

```python
import functools

import numpy as np
import jax
import jax.numpy as jnp
from jax import lax
from jax.experimental import pallas as pl
from jax.experimental.pallas import tpu as pltpu

F32 = jnp.float32
BF16 = jnp.bfloat16

D_MODEL = 1024
N_BRANCH = 4
BRANCH_W = 512
EPS = 1e-6
ROPE_THETA = 10000.0
CHUNK = 128
CONV_DIM = 512
CONV_W = 3
RET_HEADS = 4
RET_DK = 64
RET_DV = 128
SG_DIM = 512
SG_GROUPS = 4
MLA_HEADS = 8
MLA_NOPE = 64
MLA_ROPE = 32
MLA_V = 64
MLA_Q_RANK = 384
MLA_KV_RANK = 256
D_FF = 2816

LANES = 128
MLA_HEAD_PAD = 128
CONV_HALO = 8
FFN_COLS = 256
VMEM_LIMIT = 60 * 1024 * 1024
QK_SCALE_LOG2E = float((MLA_NOPE + MLA_ROPE) ** -0.5 * np.log2(np.e))

TS_FRONT = 512
TQ_ATTN = 256
TM_BACK = 512
TS_TABLE = 512

_OFF = [int(v) for v in np.cumsum([0, 512, 512, 512, 256, 256, 512, 512, 512, 512, 384, 256, 32, 4096])]
W_MAIN_COLS = _OFF[9]
W_MLA_COLS = MLA_Q_RANK + MLA_KV_RANK + MLA_HEAD_PAD
GATE_ORDER = (2, 3, 0, 1)


def _dot(a, b):
    return jnp.dot(a, b, preferred_element_type=F32)


def _dot_nt(a, b):
    return lax.dot_general(a, b, (((1,), (1,)), ((), ())), preferred_element_type=F32)


def _rms(x, g):
    return x * lax.rsqrt(jnp.mean(x * x, axis=-1, keepdims=True) + EPS) * g


def _sigmoid(x):
    return 1.0 / (1.0 + jnp.exp(-x))


def _rot_half(x, half):
    n = x.shape[-1]
    lane = lax.broadcasted_iota(jnp.int32, x.shape, x.ndim - 1)
    fwd = pltpu.roll(x, n - half, axis=x.ndim - 1)
    bwd = pltpu.roll(x, half, axis=x.ndim - 1)
    return jnp.where((lane & (2 * half - 1)) < half, fwd, bwd)


def _layer_spec(arr, layer, lead=()):
    rest = arr.shape[1 + len(lead):]
    idx = (layer,) + tuple(lead) + (0,) * len(rest)
    return pl.BlockSpec((None,) * (1 + len(lead)) + tuple(rest), lambda *_: idx,
                        pipeline_mode=pl.Buffered(1))


def _const_spec(arr):
    nd = arr.ndim
    return pl.BlockSpec(arr.shape, lambda *_: (0,) * nd, pipeline_mode=pl.Buffered(1))


def _rope_table_kernel(pos_ref, inv_ref, tab_ref):
    pos = pos_ref[0]
    ang = pos * inv_ref[0:1, :]
    c, s = jnp.cos(ang), jnp.sin(ang)
    lane = lax.broadcasted_iota(jnp.int32, ang.shape, 1)
    n_r, n_m = RET_DK // 2, MLA_ROPE // 2

    def spread_r(t):
        t = jnp.where(lane < n_r, t, pltpu.roll(t, n_r, axis=1))
        return jnp.where(lane < 2 * n_r, t, pltpu.roll(t, 2 * n_r, axis=1))

    def spread_m(t, fill):
        lo = pltpu.roll(t, MLA_NOPE - n_r, axis=1)
        hi = pltpu.roll(t, MLA_NOPE - n_r + n_m, axis=1)
        return jnp.where(lane < MLA_NOPE, fill,
                         jnp.where(lane < MLA_NOPE + n_m, lo,
                                   jnp.where(lane < MLA_NOPE + MLA_ROPE, hi, fill)))

    tab_ref[0, :, 0:128] = spread_r(c)
    tab_ref[0, :, 128:256] = spread_r(s) * inv_ref[2:3, :]
    tab_ref[0, :, 256:384] = spread_m(c, 1.0)
    tab_ref[0, :, 384:512] = spread_m(s, 0.0) * inv_ref[3:4, :]


def _rope_tables(positions):
    bsz, seq = positions.shape
    inv_r = ROPE_THETA ** (-jnp.arange(0, RET_DK, 2, dtype=F32) / RET_DK)
    inv_m = ROPE_THETA ** (-jnp.arange(0, MLA_ROPE, 2, dtype=F32) / MLA_ROPE)
    pad = MLA_HEAD_PAD - MLA_NOPE - MLA_ROPE
    lane_f = jnp.concatenate([inv_r, inv_m, jnp.zeros((LANES - inv_r.shape[0] - inv_m.shape[0],), F32)])
    sgn_r = np.tile(np.repeat(np.array([-1.0, 1.0], np.float32), RET_DK // 2), LANES // RET_DK)
    sgn_m = np.concatenate([np.zeros(MLA_NOPE, np.float32),
                            np.repeat(np.array([-1.0, 1.0], np.float32), MLA_ROPE // 2),
                            np.zeros(pad, np.float32)])
    inv = jnp.zeros((8, LANES), F32).at[0].set(lane_f).at[2].set(sgn_r).at[3].set(sgn_m)
    pos_b = jnp.broadcast_to(positions.astype(F32)[:, :, None], (bsz, seq, LANES))
    ts = min(TS_TABLE, seq)
    return pl.pallas_call(
        _rope_table_kernel,
        out_shape=jax.ShapeDtypeStruct((bsz, seq, 4 * LANES), F32),
        grid=(bsz, seq // ts),
        in_specs=[pl.BlockSpec((1, ts, LANES), lambda b, s: (b, s, 0)),
                  pl.BlockSpec((8, LANES), lambda b, s: (0, 0))],
        out_specs=pl.BlockSpec((1, ts, 4 * LANES), lambda b, s: (b, s, 0)),
        compiler_params=pltpu.CompilerParams(dimension_semantics=("arbitrary", "arbitrary")),
        name="rope_tables",
    )(pos_b, inv)


def _mix_front_kernel(x_ref, tab_ref, nrm_ref, wmain_ref, wm_ref, wg_ref, bg_ref,
                      cw_ref, lng_ref, lnb_ref, sgw_ref, sgb_ref, qn_ref, wq_ref,
                      kvn_ref, wkv_ref, wb_ref, dec_ref, xi_ref, zeta_ref,
                      mp_ref, gm_ref, q_ref, k_ref, v_ref,
                      conv_sc, state_sc, *, ts, chunk_decay):
    s_idx = pl.program_id(1)

    @pl.when(s_idx == 0)
    def _():
        conv_sc[0:CONV_HALO, :] = jnp.zeros((CONV_HALO, CONV_DIM), F32)
        state_sc[...] = jnp.zeros_like(state_sc)

    x = x_ref[0]
    hb = _rms(x, nrm_ref[...]).astype(BF16)
    tab = tab_ref[0]
    cr, sr = tab[:, 0:128], tab[:, 128:256]
    cm, sm = tab[:, 256:384], tab[:, 384:512]

    n_chunks = ts // CHUNK
    half_d = D_MODEL // 2
    rows_of = lambda c: slice(c * CHUNK, (c + 1) * CHUNK)

    def proj(ref, lo):
        return lambda: _dot(hb, ref[:, lo:lo + FFN_COLS])

    def gate_piece(i, lo):
        return lambda: _sigmoid(_dot(hb, wg_ref[:, i * D_MODEL + lo:i * D_MODEL + lo + FFN_COLS])
                                + bg_ref[i:i + 1, lo:lo + FFN_COLS])

    def gate_m_piece(lo):
        def fn():
            gm_ref[0, :, lo:lo + FFN_COLS] = gate_piece(N_BRANCH - 1, lo)().astype(BF16)
        return fn

    def pieces(name, make, lo, hi):
        return [(name, make(c)) for c in range(lo, hi, FFN_COLS)]

    queue = (pieces("rqk", lambda c: proj(wmain_ref, c), 1536, 2048)
             + pieces("rv", lambda c: proj(wmain_ref, c), 2048, 2560)
             + pieces("rg", lambda c: proj(wmain_ref, c), 2560, 3072)
             + pieces("ab", lambda c: proj(wmain_ref, c), 0, 512)
             + pieces("ac", lambda c: proj(wmain_ref, c), 512, 1024)
             + pieces("ax", lambda c: proj(wmain_ref, c), 1024, 1536)
             + pieces("gs", lambda c: gate_piece(2, c), 0, D_MODEL)
             + pieces("gm", gate_m_piece, 0, D_MODEL)
             + pieces("ga", lambda c: gate_piece(0, c), 0, D_MODEL)
             + pieces("gr", lambda c: gate_piece(1, c), 0, D_MODEL))
    done = {}

    def emit(n=1):
        for _ in range(min(n, len(queue))):
            name, fn = queue.pop(0)
            done.setdefault(name, []).append(fn())

    def need(name):
        while any(nm == name for nm, _ in queue):
            emit()
        return jnp.concatenate(done[name], axis=1)

    ps = _dot(hb, wmain_ref[:, 3072:4096])
    pm = _dot(hb, wm_ref[...])

    row = lax.broadcasted_iota(jnp.int32, (CHUNK, CHUNK), 0)
    col = lax.broadcasted_iota(jnp.int32, (CHUNK, CHUNK), 1)
    wtril = [jnp.where(row >= col, sgw_ref[g], 0.0).astype(BF16) for g in range(SG_GROUPS)]
    y_s_chunks = []
    for c in range(n_chunks):
        rows = rows_of(c)
        u = jax.nn.gelu(ps[rows, 0:512])
        sv = jax.nn.gelu(ps[rows, 512:1024])
        svc = sv - jnp.mean(sv, axis=-1, keepdims=True)
        vln = svc * lax.rsqrt(jnp.mean(svc * svc, axis=-1, keepdims=True) + EPS) * lng_ref[...] + lnb_ref[...]
        vlnb = vln.astype(BF16)
        s = jnp.concatenate([_dot(wtril[g], vlnb[:, g * CHUNK:(g + 1) * CHUNK]) + sgb_ref[g]
                             for g in range(SG_GROUPS)], axis=1)
        y_s_chunks.append((u * s).astype(BF16))
        emit(2)
    y_s = jnp.concatenate(y_s_chunks, axis=0)

    cqn = _rms(pm[:, 0:MLA_Q_RANK], qn_ref[...]).astype(BF16)
    ckvn = _rms(pm[:, MLA_Q_RANK:MLA_Q_RANK + MLA_KV_RANK], kvn_ref[...]).astype(BF16)
    qa = _dot(cqn, wq_ref[...])
    kv = _dot(ckvn, wkv_ref[...])
    lane = lax.broadcasted_iota(jnp.int32, (CHUNK, MLA_HEADS * MLA_HEAD_PAD), 1) & (MLA_HEAD_PAD - 1)
    ones_col = (lane == MLA_V).astype(F32)
    for c in range(n_chunks):
        rows = rows_of(c)
        cm_c, sm_c = cm[rows], sm[rows]
        cm8 = jnp.concatenate([cm_c] * MLA_HEADS, axis=1)
        sm8 = jnp.concatenate([sm_c] * MLA_HEADS, axis=1)
        qa_c = qa[rows]
        q_ref[0, rows, :] = ((qa_c * cm8 + _rot_half(qa_c, MLA_ROPE // 2) * sm8) * QK_SCALE_LOG2E).astype(BF16)
        kp = pm[rows, MLA_Q_RANK + MLA_KV_RANK:W_MLA_COLS]
        kpe = kp * cm_c + _rot_half(kp, MLA_ROPE // 2) * sm_c
        kv_c = kv[rows]
        k_ref[0, rows, :] = jnp.where(lane < MLA_NOPE, kv_c,
                                      jnp.concatenate([kpe] * MLA_HEADS, axis=1)).astype(BF16)
        v_up = pltpu.roll(kv_c, kv_c.shape[1] - MLA_NOPE, axis=1)
        v_ref[0, rows, :] = jnp.where(lane < MLA_V, v_up, ones_col).astype(BF16)
        emit(2)

    rqk, rv, rg = need("rqk"), need("rv").astype(BF16), need("rg")
    a_b, a_p = need("ab"), need("ac") * need("ax")
    y_r_chunks = []
    y_a_chunks = []
    for c in range(n_chunks):
        rows = rows_of(c)
        cr2 = jnp.concatenate([cr[rows], cr[rows]], axis=1)
        sr2 = jnp.concatenate([sr[rows], sr[rows]], axis=1)
        rq, rk = rqk[rows, 0:256], rqk[rows, 256:512]
        qc = rq * cr2 + _rot_half(rq, RET_DK // 2) * sr2
        kc = (rk * cr2 + _rot_half(rk, RET_DK // 2) * sr2) * (RET_DK ** -0.5)
        qcb, kcb = qc.astype(BF16), kc.astype(BF16)
        qx = (qc * xi_ref[...]).astype(BF16)
        kz = kc * zeta_ref[...]
        heads = []
        for h in range(RET_HEADS):
            hs = slice(h * RET_DK, (h + 1) * RET_DK)
            vs = slice(h * RET_DV, (h + 1) * RET_DV)
            vh = rv[rows, vs]
            scores = _dot_nt(qcb[:, hs], kcb[:, hs]) * dec_ref[h]
            state = state_sc[h]
            o = _dot(scores.astype(BF16), vh) + _dot(qx[:, hs], state.astype(BF16))
            oc = o - jnp.mean(o, axis=-1, keepdims=True)
            on = oc * lax.rsqrt(jnp.mean(oc * oc, axis=-1, keepdims=True) + EPS)
            g = rg[rows, vs]
            heads.append((g * _sigmoid(g) * on).astype(BF16))
            kzt = jnp.transpose(kz[:, hs]).astype(BF16)
            state_sc[h] = chunk_decay[h] * state + _dot(kzt, vh)
            emit()
        y_r_chunks.append(jnp.concatenate(heads, axis=1))
        p = a_p[rows]
        r0 = CONV_HALO + c * CHUNK
        conv_sc[r0:r0 + CHUNK, :] = p
        p1 = conv_sc[r0 - 1:r0 - 1 + CHUNK, :]
        p2 = conv_sc[r0 - 2:r0 - 2 + CHUNK, :]
        y_a_chunks.append(
            (a_b[rows] * (cw_ref[0:1, :] * p2 + cw_ref[1:2, :] * p1 + cw_ref[2:3, :] * p)).astype(BF16))
    conv_sc[0:CONV_HALO, :] = a_p[ts - CONV_HALO:ts, :]
    y_r = jnp.concatenate(y_r_chunks, axis=0)
    y_a = jnp.concatenate(y_a_chunks, axis=0)
    emit(len(queue))

    g_s, g_a, g_r = need("gs"), need("ga"), need("gr")
    for j in range(2):
        cols = slice(j * half_d, (j + 1) * half_d)
        mp = (g_s[:, cols] * _dot(y_s, wb_ref[2, :, cols]) + g_a[:, cols] * _dot(y_a, wb_ref[0, :, cols])
              + g_r[:, cols] * _dot(y_r, wb_ref[1, :, cols]))
        mp_ref[0, :, cols] = mp.astype(BF16)


def _retention_tables():
    log_gamma = np.log1p(-np.exp2(-5.0 - np.arange(RET_HEADS, dtype=np.float64)))
    idx = np.arange(CHUNK, dtype=np.float64)
    diff = idx[:, None] - idx[None, :]
    decay = np.where(diff >= 0, np.exp(np.maximum(diff, 0.0)[None] * log_gamma[:, None, None]), 0.0)
    zeta = np.exp((CHUNK - 1 - idx)[:, None] * log_gamma[None, :])
    xi = np.exp((idx + 1.0)[:, None] * log_gamma[None, :])
    zeta_full = np.repeat(zeta, RET_DK, axis=1)
    xi_full = np.repeat(xi, RET_DK, axis=1)
    chunk_decay = tuple(float(v) for v in np.exp(CHUNK * log_gamma))
    return (jnp.asarray(decay, F32), jnp.asarray(xi_full, F32), jnp.asarray(zeta_full, F32), chunk_decay)


def _split_w_in_kernel(w_ref, wmain_ref, wm_ref, wg_ref):
    o = _OFF
    rows = w_ref.shape[1]
    wmain_ref[0] = w_ref[0, :, 0:W_MAIN_COLS].astype(BF16)
    pad_l = jnp.zeros((rows, MLA_NOPE), F32)
    pad_r = jnp.zeros((rows, MLA_HEAD_PAD - MLA_NOPE - MLA_ROPE), F32)
    wm_ref[0] = jnp.concatenate([w_ref[0, :, o[9]:o[11]], pad_l, w_ref[0, :, o[11]:o[12]], pad_r],
                                axis=1).astype(BF16)
    wg_ref[0] = w_ref[0, :, o[12]:o[13]].astype(BF16)


def _split_w_in(w_in):
    depth, d, n_in = w_in.shape
    rb = 128
    widths = (W_MAIN_COLS, W_MLA_COLS, N_BRANCH * D_MODEL)
    return pl.pallas_call(
        _split_w_in_kernel,
        out_shape=[jax.ShapeDtypeStruct((depth, d, wd), BF16) for wd in widths],
        grid=(depth, d // rb),
        in_specs=[pl.BlockSpec((1, rb, n_in), lambda l, r: (l, r, 0))],
        out_specs=[pl.BlockSpec((1, rb, wd), lambda l, r: (l, r, 0)) for wd in widths],
        compiler_params=pltpu.CompilerParams(dimension_semantics=("arbitrary", "arbitrary")),
        name="split_w_in",
    )(w_in)


def _prep_weights(w_in, mla_w_uq, mla_w_ukv, w_branch, w_out, w_ffn_in, w_ffn_out):
    depth = w_in.shape[0]
    wmain, wm, wg = _split_w_in(w_in)
    dq = MLA_NOPE + MLA_ROPE
    uq = mla_w_uq.reshape(depth, MLA_Q_RANK, MLA_HEADS, dq)
    zq = jnp.zeros((depth, MLA_Q_RANK, MLA_HEADS, MLA_HEAD_PAD - dq), F32)
    wq = jnp.concatenate([uq, zq], axis=-1).reshape(depth, MLA_Q_RANK, MLA_HEADS * MLA_HEAD_PAD).astype(BF16)
    assert MLA_NOPE + MLA_V == MLA_HEAD_PAD
    return dict(wmain=wmain, wm=wm, wg=wg, wq=wq, wkv=mla_w_ukv.astype(BF16), wb=w_branch.astype(BF16),
                wo=w_out.astype(BF16), wfi=w_ffn_in.astype(BF16), wfo=w_ffn_out.astype(BF16))


def _mix_front(layer, x, tab, w, norm_mix, b_gate, conv_w, sg_ln_g, sg_ln_b, sg_ws, sgb,
               mla_q_norm, mla_kv_norm, ret_tabs):
    bsz, seq, _ = x.shape
    ts = min(TS_FRONT, seq)
    decay, xi_full, zeta_full, chunk_decay = ret_tabs
    stacked = [norm_mix, w["wmain"], w["wm"], w["wg"], b_gate, conv_w, sg_ln_g, sg_ln_b, sg_ws, sgb,
               mla_q_norm, w["wq"], mla_kv_norm, w["wkv"], w["wb"]]
    consts = [decay, xi_full, zeta_full]
    tile = lambda wd: pl.BlockSpec((1, ts, wd), lambda b, s: (b, s, 0))
    qk_w = MLA_HEADS * MLA_HEAD_PAD
    out_shape = [jax.ShapeDtypeStruct((bsz, seq, wd), BF16)
                 for wd in (D_MODEL, D_MODEL, qk_w, qk_w, qk_w)]
    return pl.pallas_call(
        functools.partial(_mix_front_kernel, ts=ts, chunk_decay=chunk_decay),
        out_shape=out_shape,
        grid=(bsz, seq // ts),
        in_specs=[tile(D_MODEL), tile(4 * LANES)] + [_layer_spec(a, layer) for a in stacked]
                 + [_const_spec(c) for c in consts],
        out_specs=[tile(s.shape[-1]) for s in out_shape],
        scratch_shapes=[pltpu.VMEM((CONV_HALO + ts, CONV_DIM), F32),
                        pltpu.VMEM((RET_HEADS, RET_DK, RET_DV), F32)],
        compiler_params=pltpu.CompilerParams(dimension_semantics=("arbitrary", "arbitrary"),
                                             vmem_limit_bytes=VMEM_LIMIT),
        name="mix_front",
    )(x, tab, *stacked, *consts)


def _attn_kernel(q_ref, k_ref, v_ref, o_ref, s_sc, mx_sc, acc_sc, *, tq):
    qi = pl.program_id(1)
    half = tq // 2
    n_pairs = qi // 2
    heads = range(MLA_HEADS)
    hp = lambda h: slice(h * MLA_HEAD_PAD, (h + 1) * MLA_HEAD_PAD)

    def lane_max(s):
        parts = [s[:, i * half:(i + 1) * half] for i in range(s.shape[1] // half)]
        while len(parts) > 1:
            parts = [jnp.maximum(a, b) for a, b in zip(parts[0::2], parts[1::2])]
        return parts[0]

    row = lax.broadcasted_iota(jnp.int32, (tq, tq), 0)
    col = lax.broadcasted_iota(jnp.int32, (tq, tq), 1)
    kd = pl.multiple_of(qi * tq, tq)
    for h in heads:
        s = _dot_nt(q_ref[0, :, hp(h)], k_ref[0, pl.ds(kd, tq), hp(h)])
        s = jnp.where(col <= row, s, -1e30)
        s_sc[qi, h] = s
        mx_sc[h] = lane_max(s)

    def scores_pair(i, carry):
        ks = pl.multiple_of(i * 2 * tq, 2 * tq)
        for h in heads:
            s = _dot_nt(q_ref[0, :, hp(h)], k_ref[0, pl.ds(ks, 2 * tq), hp(h)])
            s_sc[2 * i, h] = s[:, :tq]
            s_sc[2 * i + 1, h] = s[:, tq:]
            mx_sc[h] = jnp.maximum(mx_sc[h], lane_max(s))
        return carry

    lax.fori_loop(0, n_pairs, scores_pair, 0)

    @pl.when(qi % 2 == 1)
    def _():
        ks = pl.multiple_of((qi - 1) * tq, tq)
        for h in heads:
            s = _dot_nt(q_ref[0, :, hp(h)], k_ref[0, pl.ds(ks, tq), hp(h)])
            s_sc[qi - 1, h] = s
            mx_sc[h] = jnp.maximum(mx_sc[h], lane_max(s))

    for h in heads:
        mx_sc[h] = jnp.broadcast_to(jnp.max(mx_sc[h], axis=-1, keepdims=True), (tq, half))

    def probs(blocks, h):
        m = mx_sc[h]
        m2 = jnp.concatenate([m, m], axis=1)
        return jnp.concatenate([jnp.exp2(s_sc[j, h] - m2) for j in blocks], axis=1).astype(BF16)

    for h in heads:
        acc_sc[h] = _dot(probs([qi], h), v_ref[0, pl.ds(kd, tq), hp(h)])

    def pv_pair(i, carry):
        ks = pl.multiple_of(i * 2 * tq, 2 * tq)
        for h in heads:
            acc_sc[h] += _dot(probs([2 * i, 2 * i + 1], h), v_ref[0, pl.ds(ks, 2 * tq), hp(h)])
        return carry

    lax.fori_loop(0, n_pairs, pv_pair, 0)

    @pl.when(qi % 2 == 1)
    def _():
        ks = pl.multiple_of((qi - 1) * tq, tq)
        for h in heads:
            acc_sc[h] += _dot(probs([qi - 1], h), v_ref[0, pl.ds(ks, tq), hp(h)])

    for h2 in range(MLA_HEADS // 2):
        outs = []
        for h in (2 * h2, 2 * h2 + 1):
            acc = acc_sc[h]
            outs.append(acc[:, :MLA_V] * (1.0 / acc[:, MLA_V:MLA_V + 1]))
        o_ref[0, :, h2 * 2 * MLA_V:(h2 + 1) * 2 * MLA_V] = jnp.concatenate(outs, axis=1).astype(BF16)


def _attention(q, k, v):
    bsz, seq, qk_w = q.shape
    tq = min(TQ_ATTN, seq)
    o_w = MLA_HEADS * MLA_V
    return pl.pallas_call(
        functools.partial(_attn_kernel, tq=tq),
        out_shape=jax.ShapeDtypeStruct((bsz, seq, o_w), BF16),
        grid=(bsz, seq // tq),
        in_specs=[pl.BlockSpec((1, tq, qk_w), lambda b, i: (b, i, 0)),
                  pl.BlockSpec((1, seq, qk_w), lambda b, i: (b, 0, 0)),
                  pl.BlockSpec((1, seq, qk_w), lambda b, i: (b, 0, 0))],
        out_specs=pl.BlockSpec((1, tq, o_w), lambda b, i: (b, i, 0)),
        scratch_shapes=[pltpu.VMEM((seq // tq, MLA_HEADS, tq, tq), F32),
                        pltpu.VMEM((MLA_HEADS, tq, tq // 2), F32),
                        pltpu.VMEM((MLA_HEADS, tq, MLA_HEAD_PAD), F32)],
        compiler_params=pltpu.CompilerParams(dimension_semantics=("arbitrary", "arbitrary"),
                                             vmem_limit_bytes=VMEM_LIMIT),
        name="mla_attention",
    )(q, k, v)


def _mix_back_kernel(x_ref, mp_ref, gm_ref, ym_ref, wbm_ref, wo_ref, nf_ref, wfi_ref, wfo_ref,
                     fn_ref, o_ref, *, final):
    merged = mp_ref[...].astype(F32) + gm_ref[...].astype(F32) * _dot(ym_ref[...], wbm_ref[...])
    x1 = x_ref[...] + _dot(merged.astype(BF16), wo_ref[...])
    hn = _rms(x1, nf_ref[...]).astype(BF16)
    acc = jnp.zeros_like(x1)
    for c in range(D_FF // FFN_COLS):
        g = _dot(hn, wfi_ref[:, c * FFN_COLS:(c + 1) * FFN_COLS])
        u = _dot(hn, wfi_ref[:, D_FF + c * FFN_COLS:D_FF + (c + 1) * FFN_COLS])
        act = (g * _sigmoid(g) * u).astype(BF16)
        acc = acc + _dot(act, wfo_ref[c * FFN_COLS:(c + 1) * FFN_COLS, :])
    x2 = x1 + acc
    if final:
        x2 = _rms(x2, fn_ref[...])
    o_ref[...] = x2


def _mix_back(layer, x, mp, gm, ym, w, norm_ffn, final_norm, final):
    t = x.shape[0]
    tm = min(TM_BACK, t)
    tile = lambda wd: pl.BlockSpec((tm, wd), lambda i: (i, 0))
    return pl.pallas_call(
        functools.partial(_mix_back_kernel, final=final),
        out_shape=jax.ShapeDtypeStruct((t, D_MODEL), F32),
        grid=(t // tm,),
        in_specs=[tile(D_MODEL), tile(D_MODEL), tile(D_MODEL), tile(BRANCH_W),
                  _layer_spec(w["wb"], layer, lead=(N_BRANCH - 1,)), _layer_spec(w["wo"], layer),
                  _layer_spec(norm_ffn, layer), _layer_spec(w["wfi"], layer), _layer_spec(w["wfo"], layer),
                  _const_spec(final_norm)],
        out_specs=tile(D_MODEL),
        compiler_params=pltpu.CompilerParams(dimension_semantics=("arbitrary",),
                                             vmem_limit_bytes=VMEM_LIMIT),
        name="mix_back_ffn",
    )(x, mp, gm, ym, w["wb"], w["wo"], norm_ffn, w["wfi"], w["wfo"], final_norm)


def kernel(x, positions, norm_mix, w_in, b_gate, conv_w, sg_ln_g, sg_ln_b, sg_ws, sg_b, mla_q_norm,
           mla_w_uq, mla_kv_norm, mla_w_ukv, w_branch, w_out, norm_ffn, w_ffn_in, w_ffn_out, final_norm):
    bsz, seq, d = x.shape
    depth = w_in.shape[0]
    tab = _rope_tables(positions)
    w = _prep_weights(w_in, mla_w_uq, mla_w_ukv, w_branch, w_out, w_ffn_in, w_ffn_out)
    ret_tabs = _retention_tables()
    row = lambda a: a.reshape(depth, 1, a.shape[-1])
    sgb = jnp.broadcast_to(sg_b[:, :, :, None], (depth, SG_GROUPS, CHUNK, CHUNK))
    norm_mix, sg_ln_g, sg_ln_b = row(norm_mix), row(sg_ln_g), row(sg_ln_b)
    mla_q_norm, mla_kv_norm, norm_ffn = row(mla_q_norm), row(mla_kv_norm), row(norm_ffn)
    final_norm = final_norm.reshape(1, d)
    flat = lambda a: a.reshape(bsz * seq, a.shape[-1])
    for l in range(depth):
        mp, gm, q, k, v = _mix_front(l, x, tab, w, norm_mix, b_gate, conv_w, sg_ln_g, sg_ln_b, sg_ws, sgb,
                                     mla_q_norm, mla_kv_norm, ret_tabs)
        ym = _attention(q, k, v)
        x = _mix_back(l, flat(x), flat(mp), flat(gm), flat(ym), w, norm_ffn, final_norm,
                      final=(l == depth - 1)).reshape(bsz, seq, d)
    return x
```

```python
import functools

import numpy as np
import jax
import jax.numpy as jnp
from jax import lax
from jax.experimental import pallas as pl
from jax.experimental.pallas import tpu as pltpu

F32 = jnp.float32
BF16 = jnp.bfloat16

D_MODEL = 1024
N_BRANCH = 4
BRANCH_W = 512
EPS = 1e-6
ROPE_THETA = 10000.0
CHUNK = 128
CONV_DIM = 512
CONV_W = 3
RET_HEADS = 4
RET_DK = 64
RET_DV = 128
SG_DIM = 512
SG_GROUPS = 4
MLA_HEADS = 8
MLA_NOPE = 64
MLA_ROPE = 32
MLA_V = 64
MLA_Q_RANK = 384
MLA_KV_RANK = 256
D_FF = 2816

LANES = 128
MLA_HEAD_PAD = 128
CONV_HALO = 8
FFN_COLS = 256
VMEM_LIMIT = 60 * 1024 * 1024
QK_SCALE_LOG2E = float((MLA_NOPE + MLA_ROPE) ** -0.5 * np.log2(np.e))

TS_FRONT = 512
TQ_ATTN = 256
TM_BACK = 512
TS_TABLE = 512

_OFF = [int(v) for v in np.cumsum([0, 512, 512, 512, 256, 256, 512, 512, 512, 512, 384, 256, 32, 4096])]
W_MAIN_COLS = _OFF[9]
W_MLA_COLS = MLA_Q_RANK + MLA_KV_RANK + MLA_HEAD_PAD
GATE_ORDER = (2, 3, 0, 1)


def _dot(a, b):
    return jnp.dot(a, b, preferred_element_type=F32)


def _dot_nt(a, b):
    return lax.dot_general(a, b, (((1,), (1,)), ((), ())), preferred_element_type=F32)


def _rms(x, g):
    return x * lax.rsqrt(jnp.mean(x * x, axis=-1, keepdims=True) + EPS) * g


def _sigmoid(x):
    return 1.0 / (1.0 + jnp.exp(-x))


def _rot_half(x, half):
    n = x.shape[-1]
    lane = lax.broadcasted_iota(jnp.int32, x.shape, x.ndim - 1)
    fwd = pltpu.roll(x, n - half, axis=x.ndim - 1)
    bwd = pltpu.roll(x, half, axis=x.ndim - 1)
    return jnp.where((lane & (2 * half - 1)) < half, fwd, bwd)


def _layer_spec(arr, layer, lead=()):
    rest = arr.shape[1 + len(lead):]
    idx = (layer,) + tuple(lead) + (0,) * len(rest)
    return pl.BlockSpec((None,) * (1 + len(lead)) + tuple(rest), lambda *_: idx,
                        pipeline_mode=pl.Buffered(1))


def _const_spec(arr):
    nd = arr.ndim
    return pl.BlockSpec(arr.shape, lambda *_: (0,) * nd, pipeline_mode=pl.Buffered(1))


def _rope_table_kernel(pos_ref, inv_ref, tab_ref):
    pos = pos_ref[0]
    ang = pos * inv_ref[0:1, :]
    c, s = jnp.cos(ang), jnp.sin(ang)
    lane = lax.broadcasted_iota(jnp.int32, ang.shape, 1)
    n_r, n_m = RET_DK // 2, MLA_ROPE // 2

    def spread_r(t):
        t = jnp.where(lane < n_r, t, pltpu.roll(t, n_r, axis=1))
        return jnp.where(lane < 2 * n_r, t, pltpu.roll(t, 2 * n_r, axis=1))

    def spread_m(t, fill):
        lo = pltpu.roll(t, MLA_NOPE - n_r, axis=1)
        hi = pltpu.roll(t, MLA_NOPE - n_r + n_m, axis=1)
        return jnp.where(lane < MLA_NOPE, fill,
                         jnp.where(lane < MLA_NOPE + n_m, lo,
                                   jnp.where(lane < MLA_NOPE + MLA_ROPE, hi, fill)))

    tab_ref[0, :, 0:128] = spread_r(c)
    tab_ref[0, :, 128:256] = spread_r(s) * inv_ref[2:3, :]
    tab_ref[0, :, 256:384] = spread_m(c, 1.0)
    tab_ref[0, :, 384:512] = spread_m(s, 0.0) * inv_ref[3:4, :]


def _rope_tables(positions):
    bsz, seq = positions.shape
    inv_r = ROPE_THETA ** (-jnp.arange(0, RET_DK, 2, dtype=F32) / RET_DK)
    inv_m = ROPE_THETA ** (-jnp.arange(0, MLA_ROPE, 2, dtype=F32) / MLA_ROPE)
    pad = MLA_HEAD_PAD - MLA_NOPE - MLA_ROPE
    lane_f = jnp.concatenate([inv_r, inv_m, jnp.zeros((LANES - inv_r.shape[0] - inv_m.shape[0],), F32)])
    sgn_r = np.tile(np.repeat(np.array([-1.0, 1.0], np.float32), RET_DK // 2), LANES // RET_DK)
    sgn_m = np.concatenate([np.zeros(MLA_NOPE, np.float32),
                            np.repeat(np.array([-1.0, 1.0], np.float32), MLA_ROPE // 2),
                            np.zeros(pad, np.float32)])
    inv = jnp.zeros((8, LANES), F32).at[0].set(lane_f).at[2].set(sgn_r).at[3].set(sgn_m)
    pos_b = jnp.broadcast_to(positions.astype(F32)[:, :, None], (bsz, seq, LANES))
    ts = min(TS_TABLE, seq)
    return pl.pallas_call(
        _rope_table_kernel,
        out_shape=jax.ShapeDtypeStruct((bsz, seq, 4 * LANES), F32),
        grid=(bsz, seq // ts),
        in_specs=[pl.BlockSpec((1, ts, LANES), lambda b, s: (b, s, 0)),
                  pl.BlockSpec((8, LANES), lambda b, s: (0, 0))],
        out_specs=pl.BlockSpec((1, ts, 4 * LANES), lambda b, s: (b, s, 0)),
        compiler_params=pltpu.CompilerParams(dimension_semantics=("arbitrary", "arbitrary")),
        name="rope_tables",
    )(pos_b, inv)


def _mix_front_kernel(x_ref, tab_ref, nrm_ref, wmain_ref, wm_ref, wg_ref, bg_ref,
                      cw_ref, lng_ref, lnb_ref, sgw_ref, sgb_ref, qn_ref, wq_ref,
                      kvn_ref, wkv_ref, wb_ref, dec_ref, xi_ref, zeta_ref,
                      mp_ref, gm_ref, q_ref, k_ref, v_ref,
                      conv_sc, state_sc, *, ts, chunk_decay):
    s_idx = pl.program_id(1)

    @pl.when(s_idx == 0)
    def _():
        conv_sc[0:CONV_HALO, :] = jnp.zeros((CONV_HALO, CONV_DIM), F32)
        state_sc[...] = jnp.zeros_like(state_sc)

    x = x_ref[0]
    hb = _rms(x, nrm_ref[...]).astype(BF16)
    tab = tab_ref[0]
    cr, sr = tab[:, 0:128], tab[:, 128:256]
    cm, sm = tab[:, 256:384], tab[:, 384:512]

    n_chunks = ts // CHUNK
    half_d = D_MODEL // 2
    rows_of = lambda c: slice(c * CHUNK, (c + 1) * CHUNK)

    def proj(ref, lo):
        return lambda: _dot(hb, ref[:, lo:lo + FFN_COLS])

    def gate_piece(i, lo):
        return lambda: _sigmoid(_dot(hb, wg_ref[:, i * D_MODEL + lo:i * D_MODEL + lo + FFN_COLS])
                                + bg_ref[i:i + 1, lo:lo + FFN_COLS])

    def gate_m_piece(lo):
        def fn():
            gm_ref[0, :, lo:lo + FFN_COLS] = gate_piece(N_BRANCH - 1, lo)().astype(BF16)
        return fn

    def pieces(name, make, lo, hi):
        return [(name, make(c)) for c in range(lo, hi, FFN_COLS)]

    queue = (pieces("rqk", lambda c: proj(wmain_ref, c), 1536, 2048)
             + pieces("rv", lambda c: proj(wmain_ref, c), 2048, 2560)
             + pieces("rg", lambda c: proj(wmain_ref, c), 2560, 3072)
             + pieces("ab", lambda c: proj(wmain_ref, c), 0, 512)
             + pieces("ac", lambda c: proj(wmain_ref, c), 512, 1024)
             + pieces("ax", lambda c: proj(wmain_ref, c), 1024, 1536)
             + pieces("gs", lambda c: gate_piece(2, c), 0, D_MODEL)
             + pieces("gm", gate_m_piece, 0, D_MODEL)
             + pieces("ga", lambda c: gate_piece(0, c), 0, D_MODEL)
             + pieces("gr", lambda c: gate_piece(1, c), 0, D_MODEL))
    done = {}

    def emit(n=1):
        for _ in range(min(n, len(queue))):
            name, fn = queue.pop(0)
            done.setdefault(name, []).append(fn())

    def need(name):
        while any(nm == name for nm, _ in queue):
            emit()
        return jnp.concatenate(done[name], axis=1)

    ps = _dot(hb, wmain_ref[:, 3072:4096])
    pm = _dot(hb, wm_ref[...])

    row = lax.broadcasted_iota(jnp.int32, (CHUNK, CHUNK), 0)
    col = lax.broadcasted_iota(jnp.int32, (CHUNK, CHUNK), 1)
    wtril = [jnp.where(row >= col, sgw_ref[g], 0.0).astype(BF16) for g in range(SG_GROUPS)]
    y_s_chunks = []
    for c in range(n_chunks):
        rows = rows_of(c)
        u = jax.nn.gelu(ps[rows, 0:512])
        sv = jax.nn.gelu(ps[rows, 512:1024])
        svc = sv - jnp.mean(sv, axis=-1, keepdims=True)
        vln = svc * lax.rsqrt(jnp.mean(svc * svc, axis=-1, keepdims=True) + EPS) * lng_ref[...] + lnb_ref[...]
        vlnb = vln.astype(BF16)
        s = jnp.concatenate([_dot(wtril[g], vlnb[:, g * CHUNK:(g + 1) * CHUNK]) + sgb_ref[g]
                             for g in range(SG_GROUPS)], axis=1)
        y_s_chunks.append((u * s).astype(BF16))
        emit(2)
    y_s = jnp.concatenate(y_s_chunks, axis=0)

    cqn = _rms(pm[:, 0:MLA_Q_RANK], qn_ref[...]).astype(BF16)
    ckvn = _rms(pm[:, MLA_Q_RANK:MLA_Q_RANK + MLA_KV_RANK], kvn_ref[...]).astype(BF16)
    qa = _dot(cqn, wq_ref[...])
    kv = _dot(ckvn, wkv_ref[...])
    lane = lax.broadcasted_iota(jnp.int32, (CHUNK, MLA_HEADS * MLA_HEAD_PAD), 1) & (MLA_HEAD_PAD - 1)
    ones_col = (lane == MLA_V).astype(F32)
    for c in range(n_chunks):
        rows = rows_of(c)
        cm_c, sm_c = cm[rows], sm[rows]
        cm8 = jnp.concatenate([cm_c] * MLA_HEADS, axis=1)
        sm8 = jnp.concatenate([sm_c] * MLA_HEADS, axis=1)
        qa_c = qa[rows]
        q_ref[0, rows, :] = ((qa_c * cm8 + _rot_half(qa_c, MLA_ROPE // 2) * sm8) * QK_SCALE_LOG2E).astype(BF16)
        kp = pm[rows, MLA_Q_RANK + MLA_KV_RANK:W_MLA_COLS]
        kpe = kp * cm_c + _rot_half(kp, MLA_ROPE // 2) * sm_c
        kv_c = kv[rows]
        k_ref[0, rows, :] = jnp.where(lane < MLA_NOPE, kv_c,
                                      jnp.concatenate([kpe] * MLA_HEADS, axis=1)).astype(BF16)
        v_up = pltpu.roll(kv_c, kv_c.shape[1] - MLA_NOPE, axis=1)
        v_ref[0, rows, :] = jnp.where(lane < MLA_V, v_up, ones_col).astype(BF16)
        emit(2)

    rqk, rv, rg = need("rqk"), need("rv").astype(BF16), need("rg")
    a_b, a_p = need("ab"), need("ac") * need("ax")
    y_r_chunks = []
    y_a_chunks = []
    for c in range(n_chunks):
        rows = rows_of(c)
        cr2 = jnp.concatenate([cr[rows], cr[rows]], axis=1)
        sr2 = jnp.concatenate([sr[rows], sr[rows]], axis=1)
        rq, rk = rqk[rows, 0:256], rqk[rows, 256:512]
        qc = rq * cr2 + _rot_half(rq, RET_DK // 2) * sr2
        kc = (rk * cr2 + _rot_half(rk, RET_DK // 2) * sr2) * (RET_DK ** -0.5)
        qcb, kcb = qc.astype(BF16), kc.astype(BF16)
        qx = (qc * xi_ref[...]).astype(BF16)
        kz = kc * zeta_ref[...]
        heads = []
        for h in range(RET_HEADS):
            hs = slice(h * RET_DK, (h + 1) * RET_DK)
            vs = slice(h * RET_DV, (h + 1) * RET_DV)
            vh = rv[rows, vs]
            scores = _dot_nt(qcb[:, hs], kcb[:, hs]) * dec_ref[h]
            state = state_sc[h]
            o = _dot(scores.astype(BF16), vh) + _dot(qx[:, hs], state.astype(BF16))
            oc = o - jnp.mean(o, axis=-1, keepdims=True)
            on = oc * lax.rsqrt(jnp.mean(oc * oc, axis=-1, keepdims=True) + EPS)
            g = rg[rows, vs]
            heads.append((g * _sigmoid(g) * on).astype(BF16))
            kzt = jnp.transpose(kz[:, hs]).astype(BF16)
            state_sc[h] = chunk_decay[h] * state + _dot(kzt, vh)
            emit()
        y_r_chunks.append(jnp.concatenate(heads, axis=1))
        p = a_p[rows]
        r0 = CONV_HALO + c * CHUNK
        conv_sc[r0:r0 + CHUNK, :] = p
        p1 = conv_sc[r0 - 1:r0 - 1 + CHUNK, :]
        p2 = conv_sc[r0 - 2:r0 - 2 + CHUNK, :]
        y_a_chunks.append(
            (a_b[rows] * (cw_ref[0:1, :] * p2 + cw_ref[1:2, :] * p1 + cw_ref[2:3, :] * p)).astype(BF16))
    conv_sc[0:CONV_HALO, :] = a_p[ts - CONV_HALO:ts, :]
    y_r = jnp.concatenate(y_r_chunks, axis=0)
    y_a = jnp.concatenate(y_a_chunks, axis=0)
    emit(len(queue))

    g_s, g_a, g_r = need("gs"), need("ga"), need("gr")
    for j in range(2):
        cols = slice(j * half_d, (j + 1) * half_d)
        mp = (g_s[:, cols] * _dot(y_s, wb_ref[2, :, cols]) + g_a[:, cols] * _dot(y_a, wb_ref[0, :, cols])
              + g_r[:, cols] * _dot(y_r, wb_ref[1, :, cols]))
        mp_ref[0, :, cols] = mp.astype(BF16)


def _retention_tables():
    log_gamma = np.log1p(-np.exp2(-5.0 - np.arange(RET_HEADS, dtype=np.float64)))
    idx = np.arange(CHUNK, dtype=np.float64)
    diff = idx[:, None] - idx[None, :]
    decay = np.where(diff >= 0, np.exp(np.maximum(diff, 0.0)[None] * log_gamma[:, None, None]), 0.0)
    zeta = np.exp((CHUNK - 1 - idx)[:, None] * log_gamma[None, :])
    xi = np.exp((idx + 1.0)[:, None] * log_gamma[None, :])
    zeta_full = np.repeat(zeta, RET_DK, axis=1)
    xi_full = np.repeat(xi, RET_DK, axis=1)
    chunk_decay = tuple(float(v) for v in np.exp(CHUNK * log_gamma))
    return (jnp.asarray(decay, F32), jnp.asarray(xi_full, F32), jnp.asarray(zeta_full, F32), chunk_decay)


def _prep_weights(w_in, mla_w_uq, mla_w_ukv, w_branch, w_out, w_ffn_in, w_ffn_out):
    depth = w_in.shape[0]
    o = _OFF
    wmain = w_in[:, :, :W_MAIN_COLS].astype(BF16)
    pad_l = jnp.zeros((depth, D_MODEL, MLA_NOPE), F32)
    pad_r = jnp.zeros((depth, D_MODEL, MLA_HEAD_PAD - MLA_NOPE - MLA_ROPE), F32)
    wm = jnp.concatenate([w_in[:, :, o[9]:o[11]], pad_l, w_in[:, :, o[11]:o[12]], pad_r], axis=-1).astype(BF16)
    wg = w_in[:, :, o[12]:o[13]].astype(BF16)
    dq = MLA_NOPE + MLA_ROPE
    uq = mla_w_uq.reshape(depth, MLA_Q_RANK, MLA_HEADS, dq)
    zq = jnp.zeros((depth, MLA_Q_RANK, MLA_HEADS, MLA_HEAD_PAD - dq), F32)
    wq = jnp.concatenate([uq, zq], axis=-1).reshape(depth, MLA_Q_RANK, MLA_HEADS * MLA_HEAD_PAD).astype(BF16)
    assert MLA_NOPE + MLA_V == MLA_HEAD_PAD
    return dict(wmain=wmain, wm=wm, wg=wg, wq=wq, wkv=mla_w_ukv.astype(BF16), wb=w_branch.astype(BF16),
                wo=w_out.astype(BF16), wfi=w_ffn_in.astype(BF16), wfo=w_ffn_out.astype(BF16))


def _mix_front(layer, x, tab, w, norm_mix, b_gate, conv_w, sg_ln_g, sg_ln_b, sg_ws, sgb,
               mla_q_norm, mla_kv_norm, ret_tabs):
    bsz, seq, _ = x.shape
    ts = min(TS_FRONT, seq)
    decay, xi_full, zeta_full, chunk_decay = ret_tabs
    stacked = [norm_mix, w["wmain"], w["wm"], w["wg"], b_gate, conv_w, sg_ln_g, sg_ln_b, sg_ws, sgb,
               mla_q_norm, w["wq"], mla_kv_norm, w["wkv"], w["wb"]]
    consts = [decay, xi_full, zeta_full]
    tile = lambda wd: pl.BlockSpec((1, ts, wd), lambda b, s: (b, s, 0))
    qk_w = MLA_HEADS * MLA_HEAD_PAD
    out_shape = [jax.ShapeDtypeStruct((bsz, seq, wd), BF16)
                 for wd in (D_MODEL, D_MODEL, qk_w, qk_w, qk_w)]
    return pl.pallas_call(
        functools.partial(_mix_front_kernel, ts=ts, chunk_decay=chunk_decay),
        out_shape=out_shape,
        grid=(bsz, seq // ts),
        in_specs=[tile(D_MODEL), tile(4 * LANES)] + [_layer_spec(a, layer) for a in stacked]
                 + [_const_spec(c) for c in consts],
        out_specs=[tile(s.shape[-1]) for s in out_shape],
        scratch_shapes=[pltpu.VMEM((CONV_HALO + ts, CONV_DIM), F32),
                        pltpu.VMEM((RET_HEADS, RET_DK, RET_DV), F32)],
        compiler_params=pltpu.CompilerParams(dimension_semantics=("arbitrary", "arbitrary"),
                                             vmem_limit_bytes=VMEM_LIMIT),
        name="mix_front",
    )(x, tab, *stacked, *consts)


def _attn_kernel(q_ref, k_ref, v_ref, o_ref, s_sc, mx_sc, acc_sc, *, tq):
    half = tq // 2
    n_pairs = pl.program_id(1)
    heads = range(MLA_HEADS)
    hp = lambda h: slice(h * MLA_HEAD_PAD, (h + 1) * MLA_HEAD_PAD)
    row = lax.broadcasted_iota(jnp.int32, (tq, tq), 0)
    col = lax.broadcasted_iota(jnp.int32, (tq, tq), 1)

    def lane_max(s):
        parts = [s[:, i * half:(i + 1) * half] for i in range(s.shape[1] // half)]
        while len(parts) > 1:
            parts = [jnp.maximum(a, b) for a, b in zip(parts[0::2], parts[1::2])]
        return parts[0]

    def probs(blocks, h):
        m = mx_sc[h]
        m2 = jnp.concatenate([m, m], axis=1)
        return jnp.concatenate([jnp.exp2(s_sc[j, h] - m2) for j in blocks], axis=1).astype(BF16)

    for t in range(2):
        qi = 2 * n_pairs + t
        rows = slice(t * tq, (t + 1) * tq)
        kd = pl.multiple_of(qi * tq, tq)

        for h in heads:
            s = _dot_nt(q_ref[0, rows, hp(h)], k_ref[0, pl.ds(kd, tq), hp(h)])
            s = jnp.where(col <= row, s, -1e30)
            s_sc[qi, h] = s
            mx_sc[h] = lane_max(s)

        def scores_pair(i, carry):
            ks = pl.multiple_of(i * 2 * tq, 2 * tq)
            for h in heads:
                s = _dot_nt(q_ref[0, rows, hp(h)], k_ref[0, pl.ds(ks, 2 * tq), hp(h)])
                s_sc[2 * i, h] = s[:, :tq]
                s_sc[2 * i + 1, h] = s[:, tq:]
                mx_sc[h] = jnp.maximum(mx_sc[h], lane_max(s))
            return carry

        lax.fori_loop(0, n_pairs, scores_pair, 0)

        if t == 1:
            ks = pl.multiple_of((qi - 1) * tq, tq)
            for h in heads:
                s = _dot_nt(q_ref[0, rows, hp(h)], k_ref[0, pl.ds(ks, tq), hp(h)])
                s_sc[qi - 1, h] = s
                mx_sc[h] = jnp.maximum(mx_sc[h], lane_max(s))

        for h in heads:
            mx_sc[h] = jnp.broadcast_to(jnp.max(mx_sc[h], axis=-1, keepdims=True), (tq, half))

        for h in heads:
            acc_sc[h] = _dot(probs([qi], h), v_ref[0, pl.ds(kd, tq), hp(h)])

        def pv_pair(i, carry):
            ks = pl.multiple_of(i * 2 * tq, 2 * tq)
            for h in heads:
                acc_sc[h] += _dot(probs([2 * i, 2 * i + 1], h), v_ref[0, pl.ds(ks, 2 * tq), hp(h)])
            return carry

        lax.fori_loop(0, n_pairs, pv_pair, 0)

        if t == 1:
            ks = pl.multiple_of((qi - 1) * tq, tq)
            for h in heads:
                acc_sc[h] += _dot(probs([qi - 1], h), v_ref[0, pl.ds(ks, tq), hp(h)])

        for h2 in range(MLA_HEADS // 2):
            outs = []
            for h in (2 * h2, 2 * h2 + 1):
                acc = acc_sc[h]
                outs.append(acc[:, :MLA_V] * (1.0 / acc[:, MLA_V:MLA_V + 1]))
            o_ref[0, rows, h2 * 2 * MLA_V:(h2 + 1) * 2 * MLA_V] = jnp.concatenate(outs, axis=1).astype(BF16)


def _attention(q, k, v):
    bsz, seq, qk_w = q.shape
    tq = min(TQ_ATTN, seq // 2)
    o_w = MLA_HEADS * MLA_V
    return pl.pallas_call(
        functools.partial(_attn_kernel, tq=tq),
        out_shape=jax.ShapeDtypeStruct((bsz, seq, o_w), BF16),
        grid=(bsz, seq // (2 * tq)),
        in_specs=[pl.BlockSpec((1, 2 * tq, qk_w), lambda b, i: (b, i, 0)),
                  pl.BlockSpec((1, seq, qk_w), lambda b, i: (b, 0, 0)),
                  pl.BlockSpec((1, seq, qk_w), lambda b, i: (b, 0, 0))],
        out_specs=pl.BlockSpec((1, 2 * tq, o_w), lambda b, i: (b, i, 0)),
        scratch_shapes=[pltpu.VMEM((seq // tq, MLA_HEADS, tq, tq), F32),
                        pltpu.VMEM((MLA_HEADS, tq, tq // 2), F32),
                        pltpu.VMEM((MLA_HEADS, tq, MLA_HEAD_PAD), F32)],
        compiler_params=pltpu.CompilerParams(dimension_semantics=("arbitrary", "arbitrary"),
                                             vmem_limit_bytes=VMEM_LIMIT),
        name="mla_attention",
    )(q, k, v)


def _mix_back_kernel(x_ref, mp_ref, gm_ref, ym_ref, wbm_ref, wo_ref, nf_ref, wfi_ref, wfo_ref,
                     fn_ref, o_ref, *, final):
    merged = mp_ref[...].astype(F32) + gm_ref[...].astype(F32) * _dot(ym_ref[...], wbm_ref[...])
    x1 = x_ref[...] + _dot(merged.astype(BF16), wo_ref[...])
    hn = _rms(x1, nf_ref[...]).astype(BF16)
    acc = jnp.zeros_like(x1)
    for c in range(D_FF // FFN_COLS):
        g = _dot(hn, wfi_ref[:, c * FFN_COLS:(c + 1) * FFN_COLS])
        u = _dot(hn, wfi_ref[:, D_FF + c * FFN_COLS:D_FF + (c + 1) * FFN_COLS])
        act = (g * _sigmoid(g) * u).astype(BF16)
        acc = acc + _dot(act, wfo_ref[c * FFN_COLS:(c + 1) * FFN_COLS, :])
    x2 = x1 + acc
    if final:
        x2 = _rms(x2, fn_ref[...])
    o_ref[...] = x2


def _mix_back(layer, x, mp, gm, ym, w, norm_ffn, final_norm, final):
    t = x.shape[0]
    tm = min(TM_BACK, t)
    tile = lambda wd: pl.BlockSpec((tm, wd), lambda i: (i, 0))
    return pl.pallas_call(
        functools.partial(_mix_back_kernel, final=final),
        out_shape=jax.ShapeDtypeStruct((t, D_MODEL), F32),
        grid=(t // tm,),
        in_specs=[tile(D_MODEL), tile(D_MODEL), tile(D_MODEL), tile(BRANCH_W),
                  _layer_spec(w["wb"], layer, lead=(N_BRANCH - 1,)), _layer_spec(w["wo"], layer),
                  _layer_spec(norm_ffn, layer), _layer_spec(w["wfi"], layer), _layer_spec(w["wfo"], layer),
                  _const_spec(final_norm)],
        out_specs=tile(D_MODEL),
        compiler_params=pltpu.CompilerParams(dimension_semantics=("arbitrary",),
                                             vmem_limit_bytes=VMEM_LIMIT),
        name="mix_back_ffn",
    )(x, mp, gm, ym, w["wb"], w["wo"], norm_ffn, w["wfi"], w["wfo"], final_norm)


def kernel(x, positions, norm_mix, w_in, b_gate, conv_w, sg_ln_g, sg_ln_b, sg_ws, sg_b, mla_q_norm,
           mla_w_uq, mla_kv_norm, mla_w_ukv, w_branch, w_out, norm_ffn, w_ffn_in, w_ffn_out, final_norm):
    bsz, seq, d = x.shape
    depth = w_in.shape[0]
    tab = _rope_tables(positions)
    w = _prep_weights(w_in, mla_w_uq, mla_w_ukv, w_branch, w_out, w_ffn_in, w_ffn_out)
    ret_tabs = _retention_tables()
    row = lambda a: a.reshape(depth, 1, a.shape[-1])
    sgb = jnp.broadcast_to(sg_b[:, :, :, None], (depth, SG_GROUPS, CHUNK, CHUNK))
    norm_mix, sg_ln_g, sg_ln_b = row(norm_mix), row(sg_ln_g), row(sg_ln_b)
    mla_q_norm, mla_kv_norm, norm_ffn = row(mla_q_norm), row(mla_kv_norm), row(norm_ffn)
    final_norm = final_norm.reshape(1, d)
    flat = lambda a: a.reshape(bsz * seq, a.shape[-1])
    for l in range(depth):
        mp, gm, q, k, v = _mix_front(l, x, tab, w, norm_mix, b_gate, conv_w, sg_ln_g, sg_ln_b, sg_ws, sgb,
                                     mla_q_norm, mla_kv_norm, ret_tabs)
        ym = _attention(q, k, v)
        x = _mix_back(l, flat(x), flat(mp), flat(gm), flat(ym), w, norm_ffn, final_norm,
                      final=(l == depth - 1)).reshape(bsz, seq, d)
    return x
```

```python
import functools

import numpy as np
import jax
import jax.numpy as jnp
from jax import lax
from jax.experimental import pallas as pl
from jax.experimental.pallas import tpu as pltpu

F32 = jnp.float32
BF16 = jnp.bfloat16

D_MODEL = 1024
N_BRANCH = 4
BRANCH_W = 512
EPS = 1e-6
ROPE_THETA = 10000.0
CHUNK = 128
CONV_DIM = 512
CONV_W = 3
RET_HEADS = 4
RET_DK = 64
RET_DV = 128
SG_DIM = 512
SG_GROUPS = 4
MLA_HEADS = 8
MLA_NOPE = 64
MLA_ROPE = 32
MLA_V = 64
MLA_Q_RANK = 384
MLA_KV_RANK = 256
D_FF = 2816

LANES = 128
MLA_HEAD_PAD = 128
CONV_HALO = 8
FFN_COLS = 256
VMEM_LIMIT = 60 * 1024 * 1024
QK_SCALE_LOG2E = float((MLA_NOPE + MLA_ROPE) ** -0.5 * np.log2(np.e))

TS_FRONT = 512
TQ_ATTN = 256
ATTN_TILES = 4
TM_BACK = 512
TS_TABLE = 512

_OFF = [int(v) for v in np.cumsum([0, 512, 512, 512, 256, 256, 512, 512, 512, 512, 384, 256, 32, 4096])]
W_MAIN_COLS = _OFF[9]
W_MLA_COLS = MLA_Q_RANK + MLA_KV_RANK + MLA_HEAD_PAD
GATE_ORDER = (2, 3, 0, 1)


def _dot(a, b):
    return jnp.dot(a, b, preferred_element_type=F32)


def _dot_nt(a, b):
    return lax.dot_general(a, b, (((1,), (1,)), ((), ())), preferred_element_type=F32)


def _rms(x, g):
    return x * lax.rsqrt(jnp.mean(x * x, axis=-1, keepdims=True) + EPS) * g


def _sigmoid(x):
    return 1.0 / (1.0 + jnp.exp(-x))


def _first_half_mask(shape, half):
    lane = lax.broadcasted_iota(jnp.int32, shape, len(shape) - 1)
    return (lane & (2 * half - 1)) < half


def _rot_half(x, half, first):
    n = x.shape[-1]
    fwd = pltpu.roll(x, n - half, axis=x.ndim - 1)
    bwd = pltpu.roll(x, half, axis=x.ndim - 1)
    return jnp.where(first, fwd, bwd)


def _layer_spec(arr, layer, lead=()):
    rest = arr.shape[1 + len(lead):]
    idx = (layer,) + tuple(lead) + (0,) * len(rest)
    return pl.BlockSpec((None,) * (1 + len(lead)) + tuple(rest), lambda *_: idx,
                        pipeline_mode=pl.Buffered(1))


def _const_spec(arr):
    nd = arr.ndim
    return pl.BlockSpec(arr.shape, lambda *_: (0,) * nd, pipeline_mode=pl.Buffered(1))


def _rope_table_kernel(pos_ref, inv_ref, tab_ref):
    pos = pos_ref[0]
    ang = pos * inv_ref[0:1, :]
    c, s = jnp.cos(ang), jnp.sin(ang)
    lane = lax.broadcasted_iota(jnp.int32, ang.shape, 1)
    n_r, n_m = RET_DK // 2, MLA_ROPE // 2

    def spread_r(t):
        t = jnp.where(lane < n_r, t, pltpu.roll(t, n_r, axis=1))
        return jnp.where(lane < 2 * n_r, t, pltpu.roll(t, 2 * n_r, axis=1))

    def spread_m(t, fill):
        lo = pltpu.roll(t, MLA_NOPE - n_r, axis=1)
        hi = pltpu.roll(t, MLA_NOPE - n_r + n_m, axis=1)
        return jnp.where(lane < MLA_NOPE, fill,
                         jnp.where(lane < MLA_NOPE + n_m, lo,
                                   jnp.where(lane < MLA_NOPE + MLA_ROPE, hi, fill)))

    tab_ref[0, :, 0:128] = spread_r(c)
    tab_ref[0, :, 128:256] = spread_r(s) * inv_ref[2:3, :]
    tab_ref[0, :, 256:384] = spread_m(c, 1.0)
    tab_ref[0, :, 384:512] = spread_m(s, 0.0) * inv_ref[3:4, :]


def _rope_tables(positions):
    bsz, seq = positions.shape
    inv_r = ROPE_THETA ** (-jnp.arange(0, RET_DK, 2, dtype=F32) / RET_DK)
    inv_m = ROPE_THETA ** (-jnp.arange(0, MLA_ROPE, 2, dtype=F32) / MLA_ROPE)
    pad = MLA_HEAD_PAD - MLA_NOPE - MLA_ROPE
    lane_f = jnp.concatenate([inv_r, inv_m, jnp.zeros((LANES - inv_r.shape[0] - inv_m.shape[0],), F32)])
    sgn_r = np.tile(np.repeat(np.array([-1.0, 1.0], np.float32), RET_DK // 2), LANES // RET_DK)
    sgn_m = np.concatenate([np.zeros(MLA_NOPE, np.float32),
                            np.repeat(np.array([-1.0, 1.0], np.float32), MLA_ROPE // 2),
                            np.zeros(pad, np.float32)])
    inv = jnp.zeros((8, LANES), F32).at[0].set(lane_f).at[2].set(sgn_r).at[3].set(sgn_m)
    pos_b = jnp.broadcast_to(positions.astype(F32)[:, :, None], (bsz, seq, LANES))
    ts = min(TS_TABLE, seq)
    return pl.pallas_call(
        _rope_table_kernel,
        out_shape=jax.ShapeDtypeStruct((bsz, seq, 4 * LANES), F32),
        grid=(bsz, seq // ts),
        in_specs=[pl.BlockSpec((1, ts, LANES), lambda b, s: (b, s, 0)),
                  pl.BlockSpec((8, LANES), lambda b, s: (0, 0))],
        out_specs=pl.BlockSpec((1, ts, 4 * LANES), lambda b, s: (b, s, 0)),
        compiler_params=pltpu.CompilerParams(dimension_semantics=("arbitrary", "arbitrary")),
        name="rope_tables",
    )(pos_b, inv)


def _mix_front_kernel(x_ref, tab_ref, nrm_ref, wmain_ref, wm_ref, wg_ref, bg_ref,
                      cw_ref, lng_ref, lnb_ref, sgw_ref, sgb_ref, qn_ref, wq_ref,
                      kvn_ref, wkv_ref, wb_ref, dec_ref, xi_ref, zeta_ref,
                      mp_ref, gm_ref, q_ref, k_ref, v_ref,
                      conv_sc, state_sc, *, ts, chunk_decay):
    s_idx = pl.program_id(1)

    @pl.when(s_idx == 0)
    def _():
        conv_sc[0:CONV_HALO, :] = jnp.zeros((CONV_HALO, CONV_DIM), F32)
        state_sc[...] = jnp.zeros_like(state_sc)

    x = x_ref[0]
    hb = _rms(x, nrm_ref[...]).astype(BF16)
    tab = tab_ref[0]
    cr, sr = tab[:, 0:128], tab[:, 128:256]
    cm, sm = tab[:, 256:384], tab[:, 384:512]

    n_chunks = ts // CHUNK
    half_d = D_MODEL // 2
    rows_of = lambda c: slice(c * CHUNK, (c + 1) * CHUNK)
    first_q = _first_half_mask((CHUNK, MLA_HEADS * MLA_HEAD_PAD), MLA_ROPE // 2)
    first_k = _first_half_mask((CHUNK, MLA_HEAD_PAD), MLA_ROPE // 2)
    first_r = _first_half_mask((CHUNK, RET_HEADS * RET_DK), RET_DK // 2)

    def proj(ref, lo):
        return lambda: _dot(hb, ref[:, lo:lo + FFN_COLS])

    def gate_piece(i, lo):
        return lambda: _sigmoid(_dot(hb, wg_ref[:, i * D_MODEL + lo:i * D_MODEL + lo + FFN_COLS])
                                + bg_ref[i:i + 1, lo:lo + FFN_COLS])

    def gate_m_piece(lo):
        def fn():
            gm_ref[0, :, lo:lo + FFN_COLS] = gate_piece(N_BRANCH - 1, lo)().astype(BF16)
        return fn

    def pieces(name, make, lo, hi):
        return [(name, make(c)) for c in range(lo, hi, FFN_COLS)]

    queue = (pieces("rqk", lambda c: proj(wmain_ref, c), 1536, 2048)
             + pieces("rv", lambda c: proj(wmain_ref, c), 2048, 2560)
             + pieces("rg", lambda c: proj(wmain_ref, c), 2560, 3072)
             + pieces("ab", lambda c: proj(wmain_ref, c), 0, 512)
             + pieces("ac", lambda c: proj(wmain_ref, c), 512, 1024)
             + pieces("ax", lambda c: proj(wmain_ref, c), 1024, 1536)
             + pieces("gs", lambda c: gate_piece(2, c), 0, D_MODEL)
             + pieces("gm", gate_m_piece, 0, D_MODEL)
             + pieces("ga", lambda c: gate_piece(0, c), 0, D_MODEL)
             + pieces("gr", lambda c: gate_piece(1, c), 0, D_MODEL))
    done = {}

    def emit(n=1):
        for _ in range(min(n, len(queue))):
            name, fn = queue.pop(0)
            done.setdefault(name, []).append(fn())

    def need(name):
        while any(nm == name for nm, _ in queue):
            emit()
        return jnp.concatenate(done[name], axis=1)

    ps = _dot(hb, wmain_ref[:, 3072:4096])
    pm = _dot(hb, wm_ref[...])

    row = lax.broadcasted_iota(jnp.int32, (CHUNK, CHUNK), 0)
    col = lax.broadcasted_iota(jnp.int32, (CHUNK, CHUNK), 1)
    wtril = [jnp.where(row >= col, sgw_ref[g], 0.0).astype(BF16) for g in range(SG_GROUPS)]
    y_s_chunks = []
    for c in range(n_chunks):
        rows = rows_of(c)
        u = jax.nn.gelu(ps[rows, 0:512])
        sv = jax.nn.gelu(ps[rows, 512:1024])
        svc = sv - jnp.mean(sv, axis=-1, keepdims=True)
        vln = svc * lax.rsqrt(jnp.mean(svc * svc, axis=-1, keepdims=True) + EPS) * lng_ref[...] + lnb_ref[...]
        vlnb = vln.astype(BF16)
        s = jnp.concatenate([_dot(wtril[g], vlnb[:, g * CHUNK:(g + 1) * CHUNK]) + sgb_ref[g]
                             for g in range(SG_GROUPS)], axis=1)
        y_s_chunks.append((u * s).astype(BF16))
        emit(2)
    y_s = jnp.concatenate(y_s_chunks, axis=0)

    cqn = _rms(pm[:, 0:MLA_Q_RANK], qn_ref[...]).astype(BF16)
    ckvn = _rms(pm[:, MLA_Q_RANK:MLA_Q_RANK + MLA_KV_RANK], kvn_ref[...]).astype(BF16)
    qa = _dot(cqn, wq_ref[...])
    kv = _dot(ckvn, wkv_ref[...])
    lane = lax.broadcasted_iota(jnp.int32, (CHUNK, MLA_HEADS * MLA_HEAD_PAD), 1) & (MLA_HEAD_PAD - 1)
    ones_col = (lane == MLA_V).astype(F32)
    for c in range(n_chunks):
        rows = rows_of(c)
        cm_c, sm_c = cm[rows], sm[rows]
        cm8 = jnp.concatenate([cm_c] * MLA_HEADS, axis=1)
        sm8 = jnp.concatenate([sm_c] * MLA_HEADS, axis=1)
        qa_c = qa[rows]
        q_ref[0, rows, :] = ((qa_c * cm8 + _rot_half(qa_c, MLA_ROPE // 2, first_q) * sm8)
                             * QK_SCALE_LOG2E).astype(BF16)
        kp = pm[rows, MLA_Q_RANK + MLA_KV_RANK:W_MLA_COLS]
        kpe = kp * cm_c + _rot_half(kp, MLA_ROPE // 2, first_k) * sm_c
        kv_c = kv[rows]
        k_ref[0, rows, :] = jnp.where(lane < MLA_NOPE, kv_c,
                                      jnp.concatenate([kpe] * MLA_HEADS, axis=1)).astype(BF16)
        v_up = pltpu.roll(kv_c, kv_c.shape[1] - MLA_NOPE, axis=1)
        v_ref[0, rows, :] = jnp.where(lane < MLA_V, v_up, ones_col).astype(BF16)
        emit(2)

    rqk, rv, rg = need("rqk"), need("rv").astype(BF16), need("rg")
    a_b, a_p = need("ab"), need("ac") * need("ax")
    y_r_chunks = []
    y_a_chunks = []
    for c in range(n_chunks):
        rows = rows_of(c)
        cr2 = jnp.concatenate([cr[rows], cr[rows]], axis=1)
        sr2 = jnp.concatenate([sr[rows], sr[rows]], axis=1)
        rq, rk = rqk[rows, 0:256], rqk[rows, 256:512]
        qc = rq * cr2 + _rot_half(rq, RET_DK // 2, first_r) * sr2
        kc = (rk * cr2 + _rot_half(rk, RET_DK // 2, first_r) * sr2) * (RET_DK ** -0.5)
        qcb, kcb = qc.astype(BF16), kc.astype(BF16)
        qx = (qc * xi_ref[...]).astype(BF16)
        kz = kc * zeta_ref[...]
        heads = []
        for h in range(RET_HEADS):
            hs = slice(h * RET_DK, (h + 1) * RET_DK)
            vs = slice(h * RET_DV, (h + 1) * RET_DV)
            vh = rv[rows, vs]
            scores = _dot_nt(qcb[:, hs], kcb[:, hs]) * dec_ref[h]
            state = state_sc[h]
            o = _dot(scores.astype(BF16), vh) + _dot(qx[:, hs], state.astype(BF16))
            oc = o - jnp.mean(o, axis=-1, keepdims=True)
            on = oc * lax.rsqrt(jnp.mean(oc * oc, axis=-1, keepdims=True) + EPS)
            g = rg[rows, vs]
            heads.append((g * _sigmoid(g) * on).astype(BF16))
            kzt = jnp.transpose(kz[:, hs]).astype(BF16)
            state_sc[h] = chunk_decay[h] * state + _dot(kzt, vh)
            emit()
        y_r_chunks.append(jnp.concatenate(heads, axis=1))
        p = a_p[rows]
        r0 = CONV_HALO + c * CHUNK
        conv_sc[r0:r0 + CHUNK, :] = p
        p1 = conv_sc[r0 - 1:r0 - 1 + CHUNK, :]
        p2 = conv_sc[r0 - 2:r0 - 2 + CHUNK, :]
        y_a_chunks.append(
            (a_b[rows] * (cw_ref[0:1, :] * p2 + cw_ref[1:2, :] * p1 + cw_ref[2:3, :] * p)).astype(BF16))
    conv_sc[0:CONV_HALO, :] = a_p[ts - CONV_HALO:ts, :]
    y_r = jnp.concatenate(y_r_chunks, axis=0)
    y_a = jnp.concatenate(y_a_chunks, axis=0)
    emit(len(queue))

    g_s, g_a, g_r = need("gs"), need("ga"), need("gr")
    for j in range(2):
        cols = slice(j * half_d, (j + 1) * half_d)
        mp = (g_s[:, cols] * _dot(y_s, wb_ref[2, :, cols]) + g_a[:, cols] * _dot(y_a, wb_ref[0, :, cols])
              + g_r[:, cols] * _dot(y_r, wb_ref[1, :, cols]))
        mp_ref[0, :, cols] = mp.astype(BF16)


def _retention_tables():
    log_gamma = np.log1p(-np.exp2(-5.0 - np.arange(RET_HEADS, dtype=np.float64)))
    idx = np.arange(CHUNK, dtype=np.float64)
    diff = idx[:, None] - idx[None, :]
    decay = np.where(diff >= 0, np.exp(np.maximum(diff, 0.0)[None] * log_gamma[:, None, None]), 0.0)
    zeta = np.exp((CHUNK - 1 - idx)[:, None] * log_gamma[None, :])
    xi = np.exp((idx + 1.0)[:, None] * log_gamma[None, :])
    zeta_full = np.repeat(zeta, RET_DK, axis=1)
    xi_full = np.repeat(xi, RET_DK, axis=1)
    chunk_decay = tuple(float(v) for v in np.exp(CHUNK * log_gamma))
    return (jnp.asarray(decay, F32), jnp.asarray(xi_full, F32), jnp.asarray(zeta_full, F32), chunk_decay)


def _prep_weights(w_in, mla_w_uq, mla_w_ukv, w_branch, w_out, w_ffn_in, w_ffn_out):
    depth = w_in.shape[0]
    o = _OFF
    wmain = w_in[:, :, :W_MAIN_COLS].astype(BF16)
    pad_l = jnp.zeros((depth, D_MODEL, MLA_NOPE), F32)
    pad_r = jnp.zeros((depth, D_MODEL, MLA_HEAD_PAD - MLA_NOPE - MLA_ROPE), F32)
    wm = jnp.concatenate([w_in[:, :, o[9]:o[11]], pad_l, w_in[:, :, o[11]:o[12]], pad_r], axis=-1).astype(BF16)
    wg = w_in[:, :, o[12]:o[13]].astype(BF16)
    dq = MLA_NOPE + MLA_ROPE
    uq = mla_w_uq.reshape(depth, MLA_Q_RANK, MLA_HEADS, dq)
    zq = jnp.zeros((depth, MLA_Q_RANK, MLA_HEADS, MLA_HEAD_PAD - dq), F32)
    wq = jnp.concatenate([uq, zq], axis=-1).reshape(depth, MLA_Q_RANK, MLA_HEADS * MLA_HEAD_PAD).astype(BF16)
    assert MLA_NOPE + MLA_V == MLA_HEAD_PAD
    return dict(wmain=wmain, wm=wm, wg=wg, wq=wq, wkv=mla_w_ukv.astype(BF16), wb=w_branch.astype(BF16),
                wo=w_out.astype(BF16), wfi=w_ffn_in.astype(BF16), wfo=w_ffn_out.astype(BF16))


def _mix_front(layer, x, tab, w, norm_mix, b_gate, conv_w, sg_ln_g, sg_ln_b, sg_ws, sgb,
               mla_q_norm, mla_kv_norm, ret_tabs):
    bsz, seq, _ = x.shape
    ts = min(TS_FRONT, seq)
    decay, xi_full, zeta_full, chunk_decay = ret_tabs
    stacked = [norm_mix, w["wmain"], w["wm"], w["wg"], b_gate, conv_w, sg_ln_g, sg_ln_b, sg_ws, sgb,
               mla_q_norm, w["wq"], mla_kv_norm, w["wkv"], w["wb"]]
    consts = [decay, xi_full, zeta_full]
    tile = lambda wd: pl.BlockSpec((1, ts, wd), lambda b, s: (b, s, 0))
    qk_w = MLA_HEADS * MLA_HEAD_PAD
    out_shape = [jax.ShapeDtypeStruct((bsz, seq, wd), BF16)
                 for wd in (D_MODEL, D_MODEL, qk_w, qk_w, qk_w)]
    return pl.pallas_call(
        functools.partial(_mix_front_kernel, ts=ts, chunk_decay=chunk_decay),
        out_shape=out_shape,
        grid=(bsz, seq // ts),
        in_specs=[tile(D_MODEL), tile(4 * LANES)] + [_layer_spec(a, layer) for a in stacked]
                 + [_const_spec(c) for c in consts],
        out_specs=[tile(s.shape[-1]) for s in out_shape],
        scratch_shapes=[pltpu.VMEM((CONV_HALO + ts, CONV_DIM), F32),
                        pltpu.VMEM((RET_HEADS, RET_DK, RET_DV), F32)],
        compiler_params=pltpu.CompilerParams(dimension_semantics=("arbitrary", "arbitrary"),
                                             vmem_limit_bytes=VMEM_LIMIT),
        name="mix_front",
    )(x, tab, *stacked, *consts)


def _attn_kernel(q_ref, k_ref, v_ref, o_ref, s_sc, mx_sc, acc_sc, *, tq, tiles):
    half = tq // 2
    heads = range(MLA_HEADS)
    hp = lambda h: slice(h * MLA_HEAD_PAD, (h + 1) * MLA_HEAD_PAD)
    row = lax.broadcasted_iota(jnp.int32, (tq, tq), 0)
    col = lax.broadcasted_iota(jnp.int32, (tq, tq), 1)

    def lane_max(s):
        parts = [s[:, i * half:(i + 1) * half] for i in range(s.shape[1] // half)]
        while len(parts) > 1:
            parts = [jnp.maximum(a, b) for a, b in zip(parts[0::2], parts[1::2])]
        return parts[0]

    def probs(blocks, h):
        m = mx_sc[h]
        m2 = jnp.concatenate([m, m], axis=1)
        return jnp.concatenate([jnp.exp2(s_sc[j, h] - m2) for j in blocks], axis=1).astype(BF16)

    for t in range(tiles):
        qi = tiles * pl.program_id(1) + t
        n_pairs = (tiles // 2) * pl.program_id(1) + t // 2
        rows = slice(t * tq, (t + 1) * tq)
        kd = pl.multiple_of(qi * tq, tq)

        for h in heads:
            s = _dot_nt(q_ref[0, rows, hp(h)], k_ref[0, pl.ds(kd, tq), hp(h)])
            s = jnp.where(col <= row, s, -1e30)
            s_sc[qi, h] = s
            mx_sc[h] = lane_max(s)

        def scores_pair(i, carry):
            ks = pl.multiple_of(i * 2 * tq, 2 * tq)
            for h in heads:
                s = _dot_nt(q_ref[0, rows, hp(h)], k_ref[0, pl.ds(ks, 2 * tq), hp(h)])
                s_sc[2 * i, h] = s[:, :tq]
                s_sc[2 * i + 1, h] = s[:, tq:]
                mx_sc[h] = jnp.maximum(mx_sc[h], lane_max(s))
            return carry

        lax.fori_loop(0, n_pairs, scores_pair, 0)

        if t % 2 == 1:
            ks = pl.multiple_of((qi - 1) * tq, tq)
            for h in heads:
                s = _dot_nt(q_ref[0, rows, hp(h)], k_ref[0, pl.ds(ks, tq), hp(h)])
                s_sc[qi - 1, h] = s
                mx_sc[h] = jnp.maximum(mx_sc[h], lane_max(s))

        for h in heads:
            mx_sc[h] = jnp.broadcast_to(jnp.max(mx_sc[h], axis=-1, keepdims=True), (tq, half))

        for h in heads:
            acc_sc[h] = _dot(probs([qi], h), v_ref[0, pl.ds(kd, tq), hp(h)])

        def pv_pair(i, carry):
            ks = pl.multiple_of(i * 2 * tq, 2 * tq)
            for h in heads:
                acc_sc[h] += _dot(probs([2 * i, 2 * i + 1], h), v_ref[0, pl.ds(ks, 2 * tq), hp(h)])
            return carry

        lax.fori_loop(0, n_pairs, pv_pair, 0)

        if t % 2 == 1:
            ks = pl.multiple_of((qi - 1) * tq, tq)
            for h in heads:
                acc_sc[h] += _dot(probs([qi - 1], h), v_ref[0, pl.ds(ks, tq), hp(h)])

        for h2 in range(MLA_HEADS // 2):
            outs = []
            for h in (2 * h2, 2 * h2 + 1):
                acc = acc_sc[h]
                outs.append(acc[:, :MLA_V] * (1.0 / acc[:, MLA_V:MLA_V + 1]))
            o_ref[0, rows, h2 * 2 * MLA_V:(h2 + 1) * 2 * MLA_V] = jnp.concatenate(outs, axis=1).astype(BF16)


def _attention(q, k, v):
    bsz, seq, qk_w = q.shape
    tq = min(TQ_ATTN, seq // 2)
    tiles = ATTN_TILES if seq % (ATTN_TILES * tq) == 0 else 2
    o_w = MLA_HEADS * MLA_V
    return pl.pallas_call(
        functools.partial(_attn_kernel, tq=tq, tiles=tiles),
        out_shape=jax.ShapeDtypeStruct((bsz, seq, o_w), BF16),
        grid=(bsz, seq // (tiles * tq)),
        in_specs=[pl.BlockSpec((1, tiles * tq, qk_w), lambda b, i: (b, i, 0)),
                  pl.BlockSpec((1, seq, qk_w), lambda b, i: (b, 0, 0)),
                  pl.BlockSpec((1, seq, qk_w), lambda b, i: (b, 0, 0))],
        out_specs=pl.BlockSpec((1, tiles * tq, o_w), lambda b, i: (b, i, 0)),
        scratch_shapes=[pltpu.VMEM((seq // tq, MLA_HEADS, tq, tq), F32),
                        pltpu.VMEM((MLA_HEADS, tq, tq // 2), F32),
                        pltpu.VMEM((MLA_HEADS, tq, MLA_HEAD_PAD), F32)],
        compiler_params=pltpu.CompilerParams(dimension_semantics=("arbitrary", "arbitrary"),
                                             vmem_limit_bytes=VMEM_LIMIT),
        name="mla_attention",
    )(q, k, v)


def _mix_back_kernel(x_ref, mp_ref, gm_ref, ym_ref, wbm_ref, wo_ref, nf_ref, wfi_ref, wfo_ref,
                     fn_ref, o_ref, *, final):
    merged = mp_ref[...].astype(F32) + gm_ref[...].astype(F32) * _dot(ym_ref[...], wbm_ref[...])
    x1 = x_ref[...] + _dot(merged.astype(BF16), wo_ref[...])
    hn = _rms(x1, nf_ref[...]).astype(BF16)
    acc = jnp.zeros_like(x1)
    for c in range(D_FF // FFN_COLS):
        g = _dot(hn, wfi_ref[:, c * FFN_COLS:(c + 1) * FFN_COLS])
        u = _dot(hn, wfi_ref[:, D_FF + c * FFN_COLS:D_FF + (c + 1) * FFN_COLS])
        act = (g * _sigmoid(g) * u).astype(BF16)
        acc = acc + _dot(act, wfo_ref[c * FFN_COLS:(c + 1) * FFN_COLS, :])
    x2 = x1 + acc
    if final:
        x2 = _rms(x2, fn_ref[...])
    o_ref[...] = x2


def _mix_back(layer, x, mp, gm, ym, w, norm_ffn, final_norm, final):
    t = x.shape[0]
    tm = min(TM_BACK, t)
    tile = lambda wd: pl.BlockSpec((tm, wd), lambda i: (i, 0))
    return pl.pallas_call(
        functools.partial(_mix_back_kernel, final=final),
        out_shape=jax.ShapeDtypeStruct((t, D_MODEL), F32),
        grid=(t // tm,),
        in_specs=[tile(D_MODEL), tile(D_MODEL), tile(D_MODEL), tile(BRANCH_W),
                  _layer_spec(w["wb"], layer, lead=(N_BRANCH - 1,)), _layer_spec(w["wo"], layer),
                  _layer_spec(norm_ffn, layer), _layer_spec(w["wfi"], layer), _layer_spec(w["wfo"], layer),
                  _const_spec(final_norm)],
        out_specs=tile(D_MODEL),
        compiler_params=pltpu.CompilerParams(dimension_semantics=("arbitrary",),
                                             vmem_limit_bytes=VMEM_LIMIT),
        name="mix_back_ffn",
    )(x, mp, gm, ym, w["wb"], w["wo"], norm_ffn, w["wfi"], w["wfo"], final_norm)


def kernel(x, positions, norm_mix, w_in, b_gate, conv_w, sg_ln_g, sg_ln_b, sg_ws, sg_b, mla_q_norm,
           mla_w_uq, mla_kv_norm, mla_w_ukv, w_branch, w_out, norm_ffn, w_ffn_in, w_ffn_out, final_norm):
    bsz, seq, d = x.shape
    depth = w_in.shape[0]
    tab = _rope_tables(positions)
    w = _prep_weights(w_in, mla_w_uq, mla_w_ukv, w_branch, w_out, w_ffn_in, w_ffn_out)
    ret_tabs = _retention_tables()
    row = lambda a: a.reshape(depth, 1, a.shape[-1])
    sgb = jnp.broadcast_to(sg_b[:, :, :, None], (depth, SG_GROUPS, CHUNK, CHUNK))
    norm_mix, sg_ln_g, sg_ln_b = row(norm_mix), row(sg_ln_g), row(sg_ln_b)
    mla_q_norm, mla_kv_norm, norm_ffn = row(mla_q_norm), row(mla_kv_norm), row(norm_ffn)
    final_norm = final_norm.reshape(1, d)
    flat = lambda a: a.reshape(bsz * seq, a.shape[-1])
    for l in range(depth):
        mp, gm, q, k, v = _mix_front(l, x, tab, w, norm_mix, b_gate, conv_w, sg_ln_g, sg_ln_b, sg_ws, sgb,
                                     mla_q_norm, mla_kv_norm, ret_tabs)
        ym = _attention(q, k, v)
        x = _mix_back(l, flat(x), flat(mp), flat(gm), flat(ym), w, norm_ffn, final_norm,
                      final=(l == depth - 1)).reshape(bsz, seq, d)
    return x
```

```python
import functools

import numpy as np
import jax
import jax.numpy as jnp
from jax import lax
from jax.experimental import pallas as pl
from jax.experimental.pallas import tpu as pltpu

F32 = jnp.float32
BF16 = jnp.bfloat16

D_MODEL = 1024
N_BRANCH = 4
BRANCH_W = 512
EPS = 1e-6
ROPE_THETA = 10000.0
CHUNK = 128
CONV_DIM = 512
CONV_W = 3
RET_HEADS = 4
RET_DK = 64
RET_DV = 128
SG_DIM = 512
SG_GROUPS = 4
MLA_HEADS = 8
MLA_NOPE = 64
MLA_ROPE = 32
MLA_V = 64
MLA_Q_RANK = 384
MLA_KV_RANK = 256
D_FF = 2816

LANES = 128
MLA_HEAD_PAD = 128
CONV_HALO = 8
FFN_COLS = 256
VMEM_LIMIT = 60 * 1024 * 1024
QK_SCALE_LOG2E = float((MLA_NOPE + MLA_ROPE) ** -0.5 * np.log2(np.e))

TS_FRONT = 512
TQ_ATTN = 256
ATTN_TILES = 8
TM_BACK = 512
TS_TABLE = 512

_OFF = [int(v) for v in np.cumsum([0, 512, 512, 512, 256, 256, 512, 512, 512, 512, 384, 256, 32, 4096])]
W_MAIN_COLS = _OFF[9]
W_MLA_COLS = MLA_Q_RANK + MLA_KV_RANK + MLA_HEAD_PAD
GATE_ORDER = (2, 3, 0, 1)


def _dot(a, b):
    return jnp.dot(a, b, preferred_element_type=F32)


def _dot_nt(a, b):
    return lax.dot_general(a, b, (((1,), (1,)), ((), ())), preferred_element_type=F32)


def _rms(x, g):
    return x * lax.rsqrt(jnp.mean(x * x, axis=-1, keepdims=True) + EPS) * g


def _sigmoid(x):
    return 1.0 / (1.0 + jnp.exp(-x))


def _first_half_mask(shape, half):
    lane = lax.broadcasted_iota(jnp.int32, shape, len(shape) - 1)
    return (lane & (2 * half - 1)) < half


def _rot_half(x, half, first):
    n = x.shape[-1]
    fwd = pltpu.roll(x, n - half, axis=x.ndim - 1)
    bwd = pltpu.roll(x, half, axis=x.ndim - 1)
    return jnp.where(first, fwd, bwd)


def _layer_spec(arr, layer, lead=()):
    rest = arr.shape[1 + len(lead):]
    idx = (layer,) + tuple(lead) + (0,) * len(rest)
    return pl.BlockSpec((None,) * (1 + len(lead)) + tuple(rest), lambda *_: idx,
                        pipeline_mode=pl.Buffered(1))


def _const_spec(arr):
    nd = arr.ndim
    return pl.BlockSpec(arr.shape, lambda *_: (0,) * nd, pipeline_mode=pl.Buffered(1))


def _rope_table_kernel(pos_ref, inv_ref, tab_ref):
    pos = pos_ref[0]
    ang = pos * inv_ref[0:1, :]
    c, s = jnp.cos(ang), jnp.sin(ang)
    lane = lax.broadcasted_iota(jnp.int32, ang.shape, 1)
    n_r, n_m = RET_DK // 2, MLA_ROPE // 2

    def spread_r(t):
        t = jnp.where(lane < n_r, t, pltpu.roll(t, n_r, axis=1))
        return jnp.where(lane < 2 * n_r, t, pltpu.roll(t, 2 * n_r, axis=1))

    def spread_m(t, fill):
        lo = pltpu.roll(t, MLA_NOPE - n_r, axis=1)
        hi = pltpu.roll(t, MLA_NOPE - n_r + n_m, axis=1)
        return jnp.where(lane < MLA_NOPE, fill,
                         jnp.where(lane < MLA_NOPE + n_m, lo,
                                   jnp.where(lane < MLA_NOPE + MLA_ROPE, hi, fill)))

    tab_ref[0, :, 0:128] = spread_r(c)
    tab_ref[0, :, 128:256] = spread_r(s) * inv_ref[2:3, :]
    tab_ref[0, :, 256:384] = spread_m(c, 1.0)
    tab_ref[0, :, 384:512] = spread_m(s, 0.0) * inv_ref[3:4, :]


def _rope_tables(positions):
    bsz, seq = positions.shape
    inv_r = ROPE_THETA ** (-jnp.arange(0, RET_DK, 2, dtype=F32) / RET_DK)
    inv_m = ROPE_THETA ** (-jnp.arange(0, MLA_ROPE, 2, dtype=F32) / MLA_ROPE)
    pad = MLA_HEAD_PAD - MLA_NOPE - MLA_ROPE
    lane_f = jnp.concatenate([inv_r, inv_m, jnp.zeros((LANES - inv_r.shape[0] - inv_m.shape[0],), F32)])
    sgn_r = np.tile(np.repeat(np.array([-1.0, 1.0], np.float32), RET_DK // 2), LANES // RET_DK)
    sgn_m = np.concatenate([np.zeros(MLA_NOPE, np.float32),
                            np.repeat(np.array([-1.0, 1.0], np.float32), MLA_ROPE // 2),
                            np.zeros(pad, np.float32)])
    inv = jnp.zeros((8, LANES), F32).at[0].set(lane_f).at[2].set(sgn_r).at[3].set(sgn_m)
    pos_b = jnp.broadcast_to(positions.astype(F32)[:, :, None], (bsz, seq, LANES))
    ts = min(TS_TABLE, seq)
    return pl.pallas_call(
        _rope_table_kernel,
        out_shape=jax.ShapeDtypeStruct((bsz, seq, 4 * LANES), F32),
        grid=(bsz, seq // ts),
        in_specs=[pl.BlockSpec((1, ts, LANES), lambda b, s: (b, s, 0)),
                  pl.BlockSpec((8, LANES), lambda b, s: (0, 0))],
        out_specs=pl.BlockSpec((1, ts, 4 * LANES), lambda b, s: (b, s, 0)),
        compiler_params=pltpu.CompilerParams(dimension_semantics=("arbitrary", "arbitrary")),
        name="rope_tables",
    )(pos_b, inv)


def _mix_front_kernel(x_ref, tab_ref, nrm_ref, wmain_ref, wm_ref, wg_ref, bg_ref,
                      cw_ref, lng_ref, lnb_ref, sgw_ref, sgb_ref, qn_ref, wq_ref,
                      kvn_ref, wkv_ref, wb_ref, dec_ref, xi_ref, zeta_ref,
                      mp_ref, gm_ref, q_ref, k_ref, v_ref,
                      conv_sc, state_sc, *, ts, chunk_decay):
    s_idx = pl.program_id(1)

    @pl.when(s_idx == 0)
    def _():
        conv_sc[0:CONV_HALO, :] = jnp.zeros((CONV_HALO, CONV_DIM), F32)
        state_sc[...] = jnp.zeros_like(state_sc)

    x = x_ref[0]
    hb = _rms(x, nrm_ref[...]).astype(BF16)
    tab = tab_ref[0]
    cr, sr = tab[:, 0:128], tab[:, 128:256]
    cm, sm = tab[:, 256:384], tab[:, 384:512]

    n_chunks = ts // CHUNK
    half_d = D_MODEL // 2
    rows_of = lambda c: slice(c * CHUNK, (c + 1) * CHUNK)
    first_q = _first_half_mask((CHUNK, MLA_HEADS * MLA_HEAD_PAD), MLA_ROPE // 2)
    first_k = _first_half_mask((CHUNK, MLA_HEAD_PAD), MLA_ROPE // 2)
    first_r = _first_half_mask((CHUNK, RET_HEADS * RET_DK), RET_DK // 2)

    def proj(ref, lo):
        return lambda: _dot(hb, ref[:, lo:lo + FFN_COLS])

    def gate_piece(i, lo):
        return lambda: _sigmoid(_dot(hb, wg_ref[:, i * D_MODEL + lo:i * D_MODEL + lo + FFN_COLS])
                                + bg_ref[i:i + 1, lo:lo + FFN_COLS])

    def gate_m_piece(lo):
        def fn():
            gm_ref[0, :, lo:lo + FFN_COLS] = gate_piece(N_BRANCH - 1, lo)().astype(BF16)
        return fn

    def pieces(name, make, lo, hi):
        return [(name, make(c)) for c in range(lo, hi, FFN_COLS)]

    queue = (pieces("gm", gate_m_piece, 0, D_MODEL)
             + pieces("rqk", lambda c: proj(wmain_ref, c), 1536, 2048)
             + pieces("rv", lambda c: proj(wmain_ref, c), 2048, 2560)
             + pieces("rg", lambda c: proj(wmain_ref, c), 2560, 3072)
             + pieces("ab", lambda c: proj(wmain_ref, c), 0, 512)
             + pieces("ac", lambda c: proj(wmain_ref, c), 512, 1024)
             + pieces("ax", lambda c: proj(wmain_ref, c), 1024, 1536)
             + pieces("gs", lambda c: gate_piece(2, c), 0, D_MODEL)
             + pieces("ga", lambda c: gate_piece(0, c), 0, D_MODEL)
             + pieces("gr", lambda c: gate_piece(1, c), 0, D_MODEL))
    done = {}

    def emit(n=1):
        for _ in range(min(n, len(queue))):
            name, fn = queue.pop(0)
            done.setdefault(name, []).append(fn())

    def need(name):
        while any(nm == name for nm, _ in queue):
            emit()
        return jnp.concatenate(done[name], axis=1)

    ps = _dot(hb, wmain_ref[:, 3072:4096])
    pm = _dot(hb, wm_ref[...])

    row = lax.broadcasted_iota(jnp.int32, (CHUNK, CHUNK), 0)
    col = lax.broadcasted_iota(jnp.int32, (CHUNK, CHUNK), 1)
    wtril = [jnp.where(row >= col, sgw_ref[g], 0.0).astype(BF16) for g in range(SG_GROUPS)]
    y_s_chunks = []
    for c in range(n_chunks):
        rows = rows_of(c)
        u = jax.nn.gelu(ps[rows, 0:512])
        sv = jax.nn.gelu(ps[rows, 512:1024])
        svc = sv - jnp.mean(sv, axis=-1, keepdims=True)
        vln = svc * lax.rsqrt(jnp.mean(svc * svc, axis=-1, keepdims=True) + EPS) * lng_ref[...] + lnb_ref[...]
        vlnb = vln.astype(BF16)
        s = jnp.concatenate([_dot(wtril[g], vlnb[:, g * CHUNK:(g + 1) * CHUNK]) + sgb_ref[g]
                             for g in range(SG_GROUPS)], axis=1)
        y_s_chunks.append((u * s).astype(BF16))
        emit(2)
    y_s = jnp.concatenate(y_s_chunks, axis=0)

    cqn = _rms(pm[:, 0:MLA_Q_RANK], qn_ref[...]).astype(BF16)
    ckvn = _rms(pm[:, MLA_Q_RANK:MLA_Q_RANK + MLA_KV_RANK], kvn_ref[...]).astype(BF16)
    qa = _dot(cqn, wq_ref[...])
    kv = _dot(ckvn, wkv_ref[...])
    lane = lax.broadcasted_iota(jnp.int32, (CHUNK, MLA_HEADS * MLA_HEAD_PAD), 1) & (MLA_HEAD_PAD - 1)
    ones_col = (lane == MLA_V).astype(F32)
    for c in range(n_chunks):
        rows = rows_of(c)
        cm_c, sm_c = cm[rows], sm[rows]
        cm8 = jnp.concatenate([cm_c] * MLA_HEADS, axis=1)
        sm8 = jnp.concatenate([sm_c] * MLA_HEADS, axis=1)
        qa_c = qa[rows]
        q_ref[0, rows, :] = ((qa_c * cm8 + _rot_half(qa_c, MLA_ROPE // 2, first_q) * sm8)
                             * QK_SCALE_LOG2E).astype(BF16)
        kp = pm[rows, MLA_Q_RANK + MLA_KV_RANK:W_MLA_COLS]
        kpe = kp * cm_c + _rot_half(kp, MLA_ROPE // 2, first_k) * sm_c
        kv_c = kv[rows]
        k_ref[0, rows, :] = jnp.where(lane < MLA_NOPE, kv_c,
                                      jnp.concatenate([kpe] * MLA_HEADS, axis=1)).astype(BF16)
        v_up = pltpu.roll(kv_c, kv_c.shape[1] - MLA_NOPE, axis=1)
        v_ref[0, rows, :] = jnp.where(lane < MLA_V, v_up, ones_col).astype(BF16)
        emit(2)

    rqk, rv, rg = need("rqk"), need("rv").astype(BF16), need("rg")
    a_b, a_p = need("ab"), need("ac") * need("ax")
    y_r_chunks = []
    y_a_chunks = []
    for c in range(n_chunks):
        rows = rows_of(c)
        cr2 = jnp.concatenate([cr[rows], cr[rows]], axis=1)
        sr2 = jnp.concatenate([sr[rows], sr[rows]], axis=1)
        rq, rk = rqk[rows, 0:256], rqk[rows, 256:512]
        qc = rq * cr2 + _rot_half(rq, RET_DK // 2, first_r) * sr2
        kc = (rk * cr2 + _rot_half(rk, RET_DK // 2, first_r) * sr2) * (RET_DK ** -0.5)
        qcb, kcb = qc.astype(BF16), kc.astype(BF16)
        qx = (qc * xi_ref[...]).astype(BF16)
        kz = kc * zeta_ref[...]
        heads = []
        for h in range(RET_HEADS):
            hs = slice(h * RET_DK, (h + 1) * RET_DK)
            vs = slice(h * RET_DV, (h + 1) * RET_DV)
            vh = rv[rows, vs]
            scores = _dot_nt(qcb[:, hs], kcb[:, hs]) * dec_ref[h]
            state = state_sc[h]
            o = _dot(scores.astype(BF16), vh) + _dot(qx[:, hs], state.astype(BF16))
            oc = o - jnp.mean(o, axis=-1, keepdims=True)
            on = oc * lax.rsqrt(jnp.mean(oc * oc, axis=-1, keepdims=True) + EPS)
            g = rg[rows, vs]
            heads.append((g * _sigmoid(g) * on).astype(BF16))
            kzt = jnp.transpose(kz[:, hs]).astype(BF16)
            state_sc[h] = chunk_decay[h] * state + _dot(kzt, vh)
            emit()
        y_r_chunks.append(jnp.concatenate(heads, axis=1))
        p = a_p[rows]
        r0 = CONV_HALO + c * CHUNK
        conv_sc[r0:r0 + CHUNK, :] = p
        p1 = conv_sc[r0 - 1:r0 - 1 + CHUNK, :]
        p2 = conv_sc[r0 - 2:r0 - 2 + CHUNK, :]
        y_a_chunks.append(
            (a_b[rows] * (cw_ref[0:1, :] * p2 + cw_ref[1:2, :] * p1 + cw_ref[2:3, :] * p)).astype(BF16))
    conv_sc[0:CONV_HALO, :] = a_p[ts - CONV_HALO:ts, :]
    y_r = jnp.concatenate(y_r_chunks, axis=0)
    y_a = jnp.concatenate(y_a_chunks, axis=0)
    emit(len(queue))

    g_s, g_a, g_r = need("gs"), need("ga"), need("gr")
    for j in range(2):
        cols = slice(j * half_d, (j + 1) * half_d)
        mp = (g_s[:, cols] * _dot(y_s, wb_ref[2, :, cols]) + g_a[:, cols] * _dot(y_a, wb_ref[0, :, cols])
              + g_r[:, cols] * _dot(y_r, wb_ref[1, :, cols]))
        mp_ref[0, :, cols] = mp.astype(BF16)


def _retention_tables():
    log_gamma = np.log1p(-np.exp2(-5.0 - np.arange(RET_HEADS, dtype=np.float64)))
    idx = np.arange(CHUNK, dtype=np.float64)
    diff = idx[:, None] - idx[None, :]
    decay = np.where(diff >= 0, np.exp(np.maximum(diff, 0.0)[None] * log_gamma[:, None, None]), 0.0)
    zeta = np.exp((CHUNK - 1 - idx)[:, None] * log_gamma[None, :])
    xi = np.exp((idx + 1.0)[:, None] * log_gamma[None, :])
    zeta_full = np.repeat(zeta, RET_DK, axis=1)
    xi_full = np.repeat(xi, RET_DK, axis=1)
    chunk_decay = tuple(float(v) for v in np.exp(CHUNK * log_gamma))
    return (jnp.asarray(decay, F32), jnp.asarray(xi_full, F32), jnp.asarray(zeta_full, F32), chunk_decay)


def _prep_weights(w_in, mla_w_uq, mla_w_ukv, w_branch, w_out, w_ffn_in, w_ffn_out):
    depth = w_in.shape[0]
    o = _OFF
    wmain = w_in[:, :, :W_MAIN_COLS].astype(BF16)
    pad_l = jnp.zeros((depth, D_MODEL, MLA_NOPE), F32)
    pad_r = jnp.zeros((depth, D_MODEL, MLA_HEAD_PAD - MLA_NOPE - MLA_ROPE), F32)
    wm = jnp.concatenate([w_in[:, :, o[9]:o[11]], pad_l, w_in[:, :, o[11]:o[12]], pad_r], axis=-1).astype(BF16)
    wg = w_in[:, :, o[12]:o[13]].astype(BF16)
    dq = MLA_NOPE + MLA_ROPE
    uq = mla_w_uq.reshape(depth, MLA_Q_RANK, MLA_HEADS, dq)
    zq = jnp.zeros((depth, MLA_Q_RANK, MLA_HEADS, MLA_HEAD_PAD - dq), F32)
    wq = jnp.concatenate([uq, zq], axis=-1).reshape(depth, MLA_Q_RANK, MLA_HEADS * MLA_HEAD_PAD).astype(BF16)
    assert MLA_NOPE + MLA_V == MLA_HEAD_PAD
    return dict(wmain=wmain, wm=wm, wg=wg, wq=wq, wkv=mla_w_ukv.astype(BF16), wb=w_branch.astype(BF16),
                wo=w_out.astype(BF16), wfi=w_ffn_in.astype(BF16), wfo=w_ffn_out.astype(BF16))


def _mix_front(layer, x, tab, w, norm_mix, b_gate, conv_w, sg_ln_g, sg_ln_b, sg_ws, sgb,
               mla_q_norm, mla_kv_norm, ret_tabs):
    bsz, seq, _ = x.shape
    ts = min(TS_FRONT, seq)
    decay, xi_full, zeta_full, chunk_decay = ret_tabs
    stacked = [norm_mix, w["wmain"], w["wm"], w["wg"], b_gate, conv_w, sg_ln_g, sg_ln_b, sg_ws, sgb,
               mla_q_norm, w["wq"], mla_kv_norm, w["wkv"], w["wb"]]
    consts = [decay, xi_full, zeta_full]
    tile = lambda wd: pl.BlockSpec((1, ts, wd), lambda b, s: (b, s, 0))
    qk_w = MLA_HEADS * MLA_HEAD_PAD
    out_shape = [jax.ShapeDtypeStruct((bsz, seq, wd), BF16)
                 for wd in (D_MODEL, D_MODEL, qk_w, qk_w, qk_w)]
    return pl.pallas_call(
        functools.partial(_mix_front_kernel, ts=ts, chunk_decay=chunk_decay),
        out_shape=out_shape,
        grid=(bsz, seq // ts),
        in_specs=[tile(D_MODEL), tile(4 * LANES)] + [_layer_spec(a, layer) for a in stacked]
                 + [_const_spec(c) for c in consts],
        out_specs=[tile(s.shape[-1]) for s in out_shape],
        scratch_shapes=[pltpu.VMEM((CONV_HALO + ts, CONV_DIM), F32),
                        pltpu.VMEM((RET_HEADS, RET_DK, RET_DV), F32)],
        compiler_params=pltpu.CompilerParams(dimension_semantics=("arbitrary", "arbitrary"),
                                             vmem_limit_bytes=VMEM_LIMIT),
        name="mix_front",
    )(x, tab, *stacked, *consts)


def _attn_kernel(q_ref, k_ref, v_ref, o_ref, s_sc, mx_sc, acc_sc, *, tq, tiles):
    half = tq // 2
    heads = range(MLA_HEADS)
    hp = lambda h: slice(h * MLA_HEAD_PAD, (h + 1) * MLA_HEAD_PAD)
    row = lax.broadcasted_iota(jnp.int32, (tq, tq), 0)
    col = lax.broadcasted_iota(jnp.int32, (tq, tq), 1)

    def lane_max(s):
        parts = [s[:, i * half:(i + 1) * half] for i in range(s.shape[1] // half)]
        while len(parts) > 1:
            parts = [jnp.maximum(a, b) for a, b in zip(parts[0::2], parts[1::2])]
        return parts[0]

    def probs(blocks, h):
        m = mx_sc[h]
        m2 = jnp.concatenate([m, m], axis=1)
        return jnp.concatenate([jnp.exp2(s_sc[j, h] - m2) for j in blocks], axis=1).astype(BF16)

    for t in range(tiles):
        qi = tiles * pl.program_id(1) + t
        n_pairs = (tiles // 2) * pl.program_id(1) + t // 2
        rows = slice(t * tq, (t + 1) * tq)
        kd = pl.multiple_of(qi * tq, tq)

        for h in heads:
            s = _dot_nt(q_ref[0, rows, hp(h)], k_ref[0, pl.ds(kd, tq), hp(h)])
            s = jnp.where(col <= row, s, -1e30)
            s_sc[qi, h] = s
            mx_sc[h] = lane_max(s)

        def scores_pair(i, carry):
            ks = pl.multiple_of(i * 2 * tq, 2 * tq)
            for h in heads:
                s = _dot_nt(q_ref[0, rows, hp(h)], k_ref[0, pl.ds(ks, 2 * tq), hp(h)])
                s_sc[2 * i, h] = s[:, :tq]
                s_sc[2 * i + 1, h] = s[:, tq:]
                mx_sc[h] = jnp.maximum(mx_sc[h], lane_max(s))
            return carry

        lax.fori_loop(0, n_pairs, scores_pair, 0)

        if t % 2 == 1:
            ks = pl.multiple_of((qi - 1) * tq, tq)
            for h in heads:
                s = _dot_nt(q_ref[0, rows, hp(h)], k_ref[0, pl.ds(ks, tq), hp(h)])
                s_sc[qi - 1, h] = s
                mx_sc[h] = jnp.maximum(mx_sc[h], lane_max(s))

        for h in heads:
            mx_sc[h] = jnp.broadcast_to(jnp.max(mx_sc[h], axis=-1, keepdims=True), (tq, half))

        for h in heads:
            acc_sc[h] = _dot(probs([qi], h), v_ref[0, pl.ds(kd, tq), hp(h)])

        def pv_pair(i, carry):
            ks = pl.multiple_of(i * 2 * tq, 2 * tq)
            for h in heads:
                acc_sc[h] += _dot(probs([2 * i, 2 * i + 1], h), v_ref[0, pl.ds(ks, 2 * tq), hp(h)])
            return carry

        lax.fori_loop(0, n_pairs, pv_pair, 0)

        if t % 2 == 1:
            ks = pl.multiple_of((qi - 1) * tq, tq)
            for h in heads:
                acc_sc[h] += _dot(probs([qi - 1], h), v_ref[0, pl.ds(ks, tq), hp(h)])

        for h2 in range(MLA_HEADS // 2):
            outs = []
            for h in (2 * h2, 2 * h2 + 1):
                acc = acc_sc[h]
                outs.append(acc[:, :MLA_V] * (1.0 / acc[:, MLA_V:MLA_V + 1]))
            o_ref[0, rows, h2 * 2 * MLA_V:(h2 + 1) * 2 * MLA_V] = jnp.concatenate(outs, axis=1).astype(BF16)


def _attention(q, k, v):
    bsz, seq, qk_w = q.shape
    tq = min(TQ_ATTN, seq // 2)
    tiles = ATTN_TILES if seq % (ATTN_TILES * tq) == 0 else 2
    o_w = MLA_HEADS * MLA_V
    return pl.pallas_call(
        functools.partial(_attn_kernel, tq=tq, tiles=tiles),
        out_shape=jax.ShapeDtypeStruct((bsz, seq, o_w), BF16),
        grid=(bsz, seq // (tiles * tq)),
        in_specs=[pl.BlockSpec((1, tiles * tq, qk_w), lambda b, i: (b, i, 0)),
                  pl.BlockSpec((1, seq, qk_w), lambda b, i: (b, 0, 0)),
                  pl.BlockSpec((1, seq, qk_w), lambda b, i: (b, 0, 0))],
        out_specs=pl.BlockSpec((1, tiles * tq, o_w), lambda b, i: (b, i, 0)),
        scratch_shapes=[pltpu.VMEM((seq // tq, MLA_HEADS, tq, tq), F32),
                        pltpu.VMEM((MLA_HEADS, tq, tq // 2), F32),
                        pltpu.VMEM((MLA_HEADS, tq, MLA_HEAD_PAD), F32)],
        compiler_params=pltpu.CompilerParams(dimension_semantics=("arbitrary", "arbitrary"),
                                             vmem_limit_bytes=VMEM_LIMIT),
        name="mla_attention",
    )(q, k, v)


def _mix_back_kernel(x_ref, mp_ref, gm_ref, ym_ref, wbm_ref, wo_ref, nf_ref, wfi_ref, wfo_ref,
                     fn_ref, o_ref, *, final):
    merged = mp_ref[...].astype(F32) + gm_ref[...].astype(F32) * _dot(ym_ref[...], wbm_ref[...])
    x1 = x_ref[...] + _dot(merged.astype(BF16), wo_ref[...])
    hn = _rms(x1, nf_ref[...]).astype(BF16)
    acc = jnp.zeros_like(x1)
    for c in range(D_FF // FFN_COLS):
        g = _dot(hn, wfi_ref[:, c * FFN_COLS:(c + 1) * FFN_COLS])
        u = _dot(hn, wfi_ref[:, D_FF + c * FFN_COLS:D_FF + (c + 1) * FFN_COLS])
        act = (g * _sigmoid(g) * u).astype(BF16)
        acc = acc + _dot(act, wfo_ref[c * FFN_COLS:(c + 1) * FFN_COLS, :])
    x2 = x1 + acc
    if final:
        x2 = _rms(x2, fn_ref[...])
    o_ref[...] = x2


def _mix_back(layer, x, mp, gm, ym, w, norm_ffn, final_norm, final):
    t = x.shape[0]
    tm = min(TM_BACK, t)
    tile = lambda wd: pl.BlockSpec((tm, wd), lambda i: (i, 0))
    return pl.pallas_call(
        functools.partial(_mix_back_kernel, final=final),
        out_shape=jax.ShapeDtypeStruct((t, D_MODEL), F32),
        grid=(t // tm,),
        in_specs=[tile(D_MODEL), tile(D_MODEL), tile(D_MODEL), tile(BRANCH_W),
                  _layer_spec(w["wb"], layer, lead=(N_BRANCH - 1,)), _layer_spec(w["wo"], layer),
                  _layer_spec(norm_ffn, layer), _layer_spec(w["wfi"], layer), _layer_spec(w["wfo"], layer),
                  _const_spec(final_norm)],
        out_specs=tile(D_MODEL),
        compiler_params=pltpu.CompilerParams(dimension_semantics=("arbitrary",),
                                             vmem_limit_bytes=VMEM_LIMIT),
        name="mix_back_ffn",
    )(x, mp, gm, ym, w["wb"], w["wo"], norm_ffn, w["wfi"], w["wfo"], final_norm)


def kernel(x, positions, norm_mix, w_in, b_gate, conv_w, sg_ln_g, sg_ln_b, sg_ws, sg_b, mla_q_norm,
           mla_w_uq, mla_kv_norm, mla_w_ukv, w_branch, w_out, norm_ffn, w_ffn_in, w_ffn_out, final_norm):
    bsz, seq, d = x.shape
    depth = w_in.shape[0]
    tab = _rope_tables(positions)
    w = _prep_weights(w_in, mla_w_uq, mla_w_ukv, w_branch, w_out, w_ffn_in, w_ffn_out)
    ret_tabs = _retention_tables()
    row = lambda a: a.reshape(depth, 1, a.shape[-1])
    sgb = jnp.broadcast_to(sg_b[:, :, :, None], (depth, SG_GROUPS, CHUNK, CHUNK))
    norm_mix, sg_ln_g, sg_ln_b = row(norm_mix), row(sg_ln_g), row(sg_ln_b)
    mla_q_norm, mla_kv_norm, norm_ffn = row(mla_q_norm), row(mla_kv_norm), row(norm_ffn)
    final_norm = final_norm.reshape(1, d)
    flat = lambda a: a.reshape(bsz * seq, a.shape[-1])
    for l in range(depth):
        mp, gm, q, k, v = _mix_front(l, x, tab, w, norm_mix, b_gate, conv_w, sg_ln_g, sg_ln_b, sg_ws, sgb,
                                     mla_q_norm, mla_kv_norm, ret_tabs)
        ym = _attention(q, k, v)
        x = _mix_back(l, flat(x), flat(mp), flat(gm), flat(ym), w, norm_ffn, final_norm,
                      final=(l == depth - 1)).reshape(bsz, seq, d)
    return x
```

```python
import functools

import numpy as np
import jax
import jax.numpy as jnp
from jax import lax
from jax.experimental import pallas as pl
from jax.experimental.pallas import tpu as pltpu

F32 = jnp.float32
BF16 = jnp.bfloat16

D_MODEL = 1024
N_BRANCH = 4
BRANCH_W = 512
EPS = 1e-6
ROPE_THETA = 10000.0
CHUNK = 128
RET_CHUNK = 256
CONV_DIM = 512
CONV_W = 3
RET_HEADS = 4
RET_DK = 64
RET_DV = 128
SG_DIM = 512
SG_GROUPS = 4
MLA_HEADS = 8
MLA_NOPE = 64
MLA_ROPE = 32
MLA_V = 64
MLA_Q_RANK = 384
MLA_KV_RANK = 256
D_FF = 2816

LANES = 128
MLA_HEAD_PAD = 128
CONV_HALO = 8
FFN_COLS = 256
VMEM_LIMIT = 60 * 1024 * 1024
QK_SCALE_LOG2E = float((MLA_NOPE + MLA_ROPE) ** -0.5 * np.log2(np.e))

TS_FRONT = 512
TQ_ATTN = 256
ATTN_TILES = 8
TM_BACK = 512
TS_TABLE = 512

_OFF = [int(v) for v in np.cumsum([0, 512, 512, 512, 256, 256, 512, 512, 512, 512, 384, 256, 32, 4096])]
W_MAIN_COLS = _OFF[9]
W_MLA_COLS = MLA_Q_RANK + MLA_KV_RANK + MLA_HEAD_PAD
GATE_ORDER = (2, 3, 0, 1)


def _dot(a, b):
    return jnp.dot(a, b, preferred_element_type=F32)


def _dot_nt(a, b):
    return lax.dot_general(a, b, (((1,), (1,)), ((), ())), preferred_element_type=F32)


def _rms(x, g):
    return x * lax.rsqrt(jnp.mean(x * x, axis=-1, keepdims=True) + EPS) * g


def _sigmoid(x):
    return 1.0 / (1.0 + jnp.exp(-x))


def _first_half_mask(shape, half):
    lane = lax.broadcasted_iota(jnp.int32, shape, len(shape) - 1)
    return (lane & (2 * half - 1)) < half


def _rot_half(x, half, first):
    n = x.shape[-1]
    fwd = pltpu.roll(x, n - half, axis=x.ndim - 1)
    bwd = pltpu.roll(x, half, axis=x.ndim - 1)
    return jnp.where(first, fwd, bwd)


def _layer_spec(arr, layer, lead=()):
    rest = arr.shape[1 + len(lead):]
    idx = (layer,) + tuple(lead) + (0,) * len(rest)
    return pl.BlockSpec((None,) * (1 + len(lead)) + tuple(rest), lambda *_: idx,
                        pipeline_mode=pl.Buffered(1))


def _const_spec(arr):
    nd = arr.ndim
    return pl.BlockSpec(arr.shape, lambda *_: (0,) * nd, pipeline_mode=pl.Buffered(1))


def _rope_table_kernel(pos_ref, inv_ref, tab_ref):
    pos = pos_ref[0]
    ang = pos * inv_ref[0:1, :]
    c, s = jnp.cos(ang), jnp.sin(ang)
    lane = lax.broadcasted_iota(jnp.int32, ang.shape, 1)
    n_r, n_m = RET_DK // 2, MLA_ROPE // 2

    def spread_r(t):
        t = jnp.where(lane < n_r, t, pltpu.roll(t, n_r, axis=1))
        return jnp.where(lane < 2 * n_r, t, pltpu.roll(t, 2 * n_r, axis=1))

    def spread_m(t, fill):
        lo = pltpu.roll(t, MLA_NOPE - n_r, axis=1)
        hi = pltpu.roll(t, MLA_NOPE - n_r + n_m, axis=1)
        return jnp.where(lane < MLA_NOPE, fill,
                         jnp.where(lane < MLA_NOPE + n_m, lo,
                                   jnp.where(lane < MLA_NOPE + MLA_ROPE, hi, fill)))

    tab_ref[0, :, 0:128] = spread_r(c)
    tab_ref[0, :, 128:256] = spread_r(s) * inv_ref[2:3, :]
    tab_ref[0, :, 256:384] = spread_m(c, 1.0)
    tab_ref[0, :, 384:512] = spread_m(s, 0.0) * inv_ref[3:4, :]


def _rope_tables(positions):
    bsz, seq = positions.shape
    inv_r = ROPE_THETA ** (-jnp.arange(0, RET_DK, 2, dtype=F32) / RET_DK)
    inv_m = ROPE_THETA ** (-jnp.arange(0, MLA_ROPE, 2, dtype=F32) / MLA_ROPE)
    pad = MLA_HEAD_PAD - MLA_NOPE - MLA_ROPE
    lane_f = jnp.concatenate([inv_r, inv_m, jnp.zeros((LANES - inv_r.shape[0] - inv_m.shape[0],), F32)])
    sgn_r = np.tile(np.repeat(np.array([-1.0, 1.0], np.float32), RET_DK // 2), LANES // RET_DK)
    sgn_m = np.concatenate([np.zeros(MLA_NOPE, np.float32),
                            np.repeat(np.array([-1.0, 1.0], np.float32), MLA_ROPE // 2),
                            np.zeros(pad, np.float32)])
    inv = jnp.zeros((8, LANES), F32).at[0].set(lane_f).at[2].set(sgn_r).at[3].set(sgn_m)
    pos_b = jnp.broadcast_to(positions.astype(F32)[:, :, None], (bsz, seq, LANES))
    ts = min(TS_TABLE, seq)
    return pl.pallas_call(
        _rope_table_kernel,
        out_shape=jax.ShapeDtypeStruct((bsz, seq, 4 * LANES), F32),
        grid=(bsz, seq // ts),
        in_specs=[pl.BlockSpec((1, ts, LANES), lambda b, s: (b, s, 0)),
                  pl.BlockSpec((8, LANES), lambda b, s: (0, 0))],
        out_specs=pl.BlockSpec((1, ts, 4 * LANES), lambda b, s: (b, s, 0)),
        compiler_params=pltpu.CompilerParams(dimension_semantics=("arbitrary", "arbitrary")),
        name="rope_tables",
    )(pos_b, inv)


def _mix_front_kernel(x_ref, tab_ref, nrm_ref, wmain_ref, wm_ref, wg_ref, bg_ref,
                      cw_ref, lng_ref, lnb_ref, sgw_ref, sgb_ref, qn_ref, wq_ref,
                      kvn_ref, wkv_ref, wb_ref, dec_ref, xi_ref, zeta_ref,
                      mp_ref, gm_ref, q_ref, k_ref, v_ref,
                      conv_sc, state_sc, *, ts, chunk_decay):
    s_idx = pl.program_id(1)

    @pl.when(s_idx == 0)
    def _():
        conv_sc[0:CONV_HALO, :] = jnp.zeros((CONV_HALO, CONV_DIM), F32)
        state_sc[...] = jnp.zeros_like(state_sc)

    x = x_ref[0]
    hb = _rms(x, nrm_ref[...]).astype(BF16)
    tab = tab_ref[0]
    cr, sr = tab[:, 0:128], tab[:, 128:256]
    cm, sm = tab[:, 256:384], tab[:, 384:512]

    n_chunks = ts // CHUNK
    half_d = D_MODEL // 2
    rows_of = lambda c: slice(c * CHUNK, (c + 1) * CHUNK)
    first_q = _first_half_mask((CHUNK, MLA_HEADS * MLA_HEAD_PAD), MLA_ROPE // 2)
    first_k = _first_half_mask((CHUNK, MLA_HEAD_PAD), MLA_ROPE // 2)
    first_r = _first_half_mask((RET_CHUNK, RET_HEADS * RET_DK), RET_DK // 2)

    def proj(ref, lo):
        return lambda: _dot(hb, ref[:, lo:lo + FFN_COLS])

    def gate_piece(i, lo):
        return lambda: _sigmoid(_dot(hb, wg_ref[:, i * D_MODEL + lo:i * D_MODEL + lo + FFN_COLS])
                                + bg_ref[i:i + 1, lo:lo + FFN_COLS])

    def gate_m_piece(lo):
        def fn():
            gm_ref[0, :, lo:lo + FFN_COLS] = gate_piece(N_BRANCH - 1, lo)().astype(BF16)
        return fn

    def pieces(name, make, lo, hi):
        return [(name, make(c)) for c in range(lo, hi, FFN_COLS)]

    queue = (pieces("gm", gate_m_piece, 0, D_MODEL)
             + pieces("rqk", lambda c: proj(wmain_ref, c), 1536, 2048)
             + pieces("rv", lambda c: proj(wmain_ref, c), 2048, 2560)
             + pieces("rg", lambda c: proj(wmain_ref, c), 2560, 3072)
             + pieces("ab", lambda c: proj(wmain_ref, c), 0, 512)
             + pieces("ac", lambda c: proj(wmain_ref, c), 512, 1024)
             + pieces("ax", lambda c: proj(wmain_ref, c), 1024, 1536)
             + pieces("gs", lambda c: gate_piece(2, c), 0, D_MODEL)
             + pieces("ga", lambda c: gate_piece(0, c), 0, D_MODEL)
             + pieces("gr", lambda c: gate_piece(1, c), 0, D_MODEL))
    done = {}

    def emit(n=1):
        for _ in range(min(n, len(queue))):
            name, fn = queue.pop(0)
            done.setdefault(name, []).append(fn())

    def need(name):
        while any(nm == name for nm, _ in queue):
            emit()
        return jnp.concatenate(done[name], axis=1)

    ps = _dot(hb, wmain_ref[:, 3072:4096])
    pm = _dot(hb, wm_ref[...])

    row = lax.broadcasted_iota(jnp.int32, (CHUNK, CHUNK), 0)
    col = lax.broadcasted_iota(jnp.int32, (CHUNK, CHUNK), 1)
    wtril = [jnp.where(row >= col, sgw_ref[g], 0.0).astype(BF16) for g in range(SG_GROUPS)]
    y_s_chunks = []
    for c in range(n_chunks):
        rows = rows_of(c)
        u = jax.nn.gelu(ps[rows, 0:512])
        sv = jax.nn.gelu(ps[rows, 512:1024])
        svc = sv - jnp.mean(sv, axis=-1, keepdims=True)
        vln = svc * lax.rsqrt(jnp.mean(svc * svc, axis=-1, keepdims=True) + EPS) * lng_ref[...] + lnb_ref[...]
        vlnb = vln.astype(BF16)
        s = jnp.concatenate([_dot(wtril[g], vlnb[:, g * CHUNK:(g + 1) * CHUNK]) + sgb_ref[g]
                             for g in range(SG_GROUPS)], axis=1)
        y_s_chunks.append((u * s).astype(BF16))
        emit(2)
    y_s = jnp.concatenate(y_s_chunks, axis=0)

    cqn = _rms(pm[:, 0:MLA_Q_RANK], qn_ref[...]).astype(BF16)
    ckvn = _rms(pm[:, MLA_Q_RANK:MLA_Q_RANK + MLA_KV_RANK], kvn_ref[...]).astype(BF16)
    qa = _dot(cqn, wq_ref[...])
    kv = _dot(ckvn, wkv_ref[...])
    lane = lax.broadcasted_iota(jnp.int32, (CHUNK, MLA_HEADS * MLA_HEAD_PAD), 1) & (MLA_HEAD_PAD - 1)
    ones_col = (lane == MLA_V).astype(F32)
    for c in range(n_chunks):
        rows = rows_of(c)
        cm_c, sm_c = cm[rows], sm[rows]
        cm8 = jnp.concatenate([cm_c] * MLA_HEADS, axis=1)
        sm8 = jnp.concatenate([sm_c] * MLA_HEADS, axis=1)
        qa_c = qa[rows]
        q_ref[0, rows, :] = ((qa_c * cm8 + _rot_half(qa_c, MLA_ROPE // 2, first_q) * sm8)
                             * QK_SCALE_LOG2E).astype(BF16)
        kp = pm[rows, MLA_Q_RANK + MLA_KV_RANK:W_MLA_COLS]
        kpe = kp * cm_c + _rot_half(kp, MLA_ROPE // 2, first_k) * sm_c
        kv_c = kv[rows]
        k_ref[0, rows, :] = jnp.where(lane < MLA_NOPE, kv_c,
                                      jnp.concatenate([kpe] * MLA_HEADS, axis=1)).astype(BF16)
        v_up = pltpu.roll(kv_c, kv_c.shape[1] - MLA_NOPE, axis=1)
        v_ref[0, rows, :] = jnp.where(lane < MLA_V, v_up, ones_col).astype(BF16)
        emit(2)

    rqk, rv, rg = need("rqk"), need("rv").astype(BF16), need("rg")
    a_b, a_p = need("ab"), need("ac") * need("ax")
    y_r_chunks = []
    y_a_chunks = []
    for c in range(ts // RET_CHUNK):
        rows = slice(c * RET_CHUNK, (c + 1) * RET_CHUNK)
        cr2 = jnp.concatenate([cr[rows], cr[rows]], axis=1)
        sr2 = jnp.concatenate([sr[rows], sr[rows]], axis=1)
        rq, rk = rqk[rows, 0:256], rqk[rows, 256:512]
        qc = rq * cr2 + _rot_half(rq, RET_DK // 2, first_r) * sr2
        kc = (rk * cr2 + _rot_half(rk, RET_DK // 2, first_r) * sr2) * (RET_DK ** -0.5)
        qcb, kcb = qc.astype(BF16), kc.astype(BF16)
        qx = (qc * xi_ref[...]).astype(BF16)
        kz = kc * zeta_ref[...]
        heads = []
        for h in range(RET_HEADS):
            hs = slice(h * RET_DK, (h + 1) * RET_DK)
            vs = slice(h * RET_DV, (h + 1) * RET_DV)
            vh = rv[rows, vs]
            scores = _dot_nt(qcb[:, hs], kcb[:, hs]) * dec_ref[h]
            state = state_sc[h]
            o = _dot(scores.astype(BF16), vh) + _dot(qx[:, hs], state.astype(BF16))
            oc = o - jnp.mean(o, axis=-1, keepdims=True)
            on = oc * lax.rsqrt(jnp.mean(oc * oc, axis=-1, keepdims=True) + EPS)
            g = rg[rows, vs]
            heads.append((g * _sigmoid(g) * on).astype(BF16))
            kzt = jnp.transpose(kz[:, hs]).astype(BF16)
            state_sc[h] = chunk_decay[h] * state + _dot(kzt, vh)
            emit(RET_CHUNK // CHUNK)
        y_r_chunks.append(jnp.concatenate(heads, axis=1))
        p = a_p[rows]
        r0 = CONV_HALO + c * RET_CHUNK
        conv_sc[r0:r0 + RET_CHUNK, :] = p
        p1 = conv_sc[r0 - 1:r0 - 1 + RET_CHUNK, :]
        p2 = conv_sc[r0 - 2:r0 - 2 + RET_CHUNK, :]
        y_a_chunks.append(
            (a_b[rows] * (cw_ref[0:1, :] * p2 + cw_ref[1:2, :] * p1 + cw_ref[2:3, :] * p)).astype(BF16))
    conv_sc[0:CONV_HALO, :] = a_p[ts - CONV_HALO:ts, :]
    y_r = jnp.concatenate(y_r_chunks, axis=0)
    y_a = jnp.concatenate(y_a_chunks, axis=0)
    emit(len(queue))

    g_s, g_a, g_r = need("gs"), need("ga"), need("gr")
    for j in range(2):
        cols = slice(j * half_d, (j + 1) * half_d)
        mp = (g_s[:, cols] * _dot(y_s, wb_ref[2, :, cols]) + g_a[:, cols] * _dot(y_a, wb_ref[0, :, cols])
              + g_r[:, cols] * _dot(y_r, wb_ref[1, :, cols]))
        mp_ref[0, :, cols] = mp.astype(BF16)


def _retention_tables():
    log_gamma = np.log1p(-np.exp2(-5.0 - np.arange(RET_HEADS, dtype=np.float64)))
    idx = np.arange(RET_CHUNK, dtype=np.float64)
    diff = idx[:, None] - idx[None, :]
    decay = np.where(diff >= 0, np.exp(np.maximum(diff, 0.0)[None] * log_gamma[:, None, None]), 0.0)
    zeta = np.exp((RET_CHUNK - 1 - idx)[:, None] * log_gamma[None, :])
    xi = np.exp((idx + 1.0)[:, None] * log_gamma[None, :])
    zeta_full = np.repeat(zeta, RET_DK, axis=1)
    xi_full = np.repeat(xi, RET_DK, axis=1)
    chunk_decay = tuple(float(v) for v in np.exp(RET_CHUNK * log_gamma))
    return (jnp.asarray(decay, F32), jnp.asarray(xi_full, F32), jnp.asarray(zeta_full, F32), chunk_decay)


def _prep_weights(w_in, mla_w_uq, mla_w_ukv, w_branch, w_out, w_ffn_in, w_ffn_out):
    depth = w_in.shape[0]
    o = _OFF
    wmain = w_in[:, :, :W_MAIN_COLS].astype(BF16)
    pad_l = jnp.zeros((depth, D_MODEL, MLA_NOPE), F32)
    pad_r = jnp.zeros((depth, D_MODEL, MLA_HEAD_PAD - MLA_NOPE - MLA_ROPE), F32)
    wm = jnp.concatenate([w_in[:, :, o[9]:o[11]], pad_l, w_in[:, :, o[11]:o[12]], pad_r], axis=-1).astype(BF16)
    wg = w_in[:, :, o[12]:o[13]].astype(BF16)
    dq = MLA_NOPE + MLA_ROPE
    uq = mla_w_uq.reshape(depth, MLA_Q_RANK, MLA_HEADS, dq)
    zq = jnp.zeros((depth, MLA_Q_RANK, MLA_HEADS, MLA_HEAD_PAD - dq), F32)
    wq = jnp.concatenate([uq, zq], axis=-1).reshape(depth, MLA_Q_RANK, MLA_HEADS * MLA_HEAD_PAD).astype(BF16)
    assert MLA_NOPE + MLA_V == MLA_HEAD_PAD
    return dict(wmain=wmain, wm=wm, wg=wg, wq=wq, wkv=mla_w_ukv.astype(BF16), wb=w_branch.astype(BF16),
                wo=w_out.astype(BF16), wfi=w_ffn_in.astype(BF16), wfo=w_ffn_out.astype(BF16))


def _mix_front(layer, x, tab, w, norm_mix, b_gate, conv_w, sg_ln_g, sg_ln_b, sg_ws, sgb,
               mla_q_norm, mla_kv_norm, ret_tabs):
    bsz, seq, _ = x.shape
    ts = min(TS_FRONT, seq)
    decay, xi_full, zeta_full, chunk_decay = ret_tabs
    stacked = [norm_mix, w["wmain"], w["wm"], w["wg"], b_gate, conv_w, sg_ln_g, sg_ln_b, sg_ws, sgb,
               mla_q_norm, w["wq"], mla_kv_norm, w["wkv"], w["wb"]]
    consts = [decay, xi_full, zeta_full]
    tile = lambda wd: pl.BlockSpec((1, ts, wd), lambda b, s: (b, s, 0))
    qk_w = MLA_HEADS * MLA_HEAD_PAD
    out_shape = [jax.ShapeDtypeStruct((bsz, seq, wd), BF16)
                 for wd in (D_MODEL, D_MODEL, qk_w, qk_w, qk_w)]
    return pl.pallas_call(
        functools.partial(_mix_front_kernel, ts=ts, chunk_decay=chunk_decay),
        out_shape=out_shape,
        grid=(bsz, seq // ts),
        in_specs=[tile(D_MODEL), tile(4 * LANES)] + [_layer_spec(a, layer) for a in stacked]
                 + [_const_spec(c) for c in consts],
        out_specs=[tile(s.shape[-1]) for s in out_shape],
        scratch_shapes=[pltpu.VMEM((CONV_HALO + ts, CONV_DIM), F32),
                        pltpu.VMEM((RET_HEADS, RET_DK, RET_DV), F32)],
        compiler_params=pltpu.CompilerParams(dimension_semantics=("arbitrary", "arbitrary"),
                                             vmem_limit_bytes=VMEM_LIMIT),
        name="mix_front",
    )(x, tab, *stacked, *consts)


def _attn_kernel(q_ref, k_ref, v_ref, o_ref, s_sc, mx_sc, acc_sc, *, tq, tiles):
    half = tq // 2
    heads = range(MLA_HEADS)
    hp = lambda h: slice(h * MLA_HEAD_PAD, (h + 1) * MLA_HEAD_PAD)
    row = lax.broadcasted_iota(jnp.int32, (tq, tq), 0)
    col = lax.broadcasted_iota(jnp.int32, (tq, tq), 1)

    def lane_max(s):
        parts = [s[:, i * half:(i + 1) * half] for i in range(s.shape[1] // half)]
        while len(parts) > 1:
            parts = [jnp.maximum(a, b) for a, b in zip(parts[0::2], parts[1::2])]
        return parts[0]

    def probs(blocks, h):
        m = mx_sc[h]
        m2 = jnp.concatenate([m, m], axis=1)
        return jnp.concatenate([jnp.exp2(s_sc[j, h] - m2) for j in blocks], axis=1).astype(BF16)

    for t in range(tiles):
        qi = tiles * pl.program_id(1) + t
        n_pairs = (tiles // 2) * pl.program_id(1) + t // 2
        rows = slice(t * tq, (t + 1) * tq)
        kd = pl.multiple_of(qi * tq, tq)

        for h in heads:
            s = _dot_nt(q_ref[0, rows, hp(h)], k_ref[0, pl.ds(kd, tq), hp(h)])
            s = jnp.where(col <= row, s, -1e30)
            s_sc[qi, h] = s
            mx_sc[h] = lane_max(s)

        def scores_pair(i, carry):
            ks = pl.multiple_of(i * 2 * tq, 2 * tq)
            for h in heads:
                s = _dot_nt(q_ref[0, rows, hp(h)], k_ref[0, pl.ds(ks, 2 * tq), hp(h)])
                s_sc[2 * i, h] = s[:, :tq]
                s_sc[2 * i + 1, h] = s[:, tq:]
                mx_sc[h] = jnp.maximum(mx_sc[h], lane_max(s))
            return carry

        lax.fori_loop(0, n_pairs, scores_pair, 0)

        if t % 2 == 1:
            ks = pl.multiple_of((qi - 1) * tq, tq)
            for h in heads:
                s = _dot_nt(q_ref[0, rows, hp(h)], k_ref[0, pl.ds(ks, tq), hp(h)])
                s_sc[qi - 1, h] = s
                mx_sc[h] = jnp.maximum(mx_sc[h], lane_max(s))

        for h in heads:
            mx_sc[h] = jnp.broadcast_to(jnp.max(mx_sc[h], axis=-1, keepdims=True), (tq, half))

        for h in heads:
            acc_sc[h] = _dot(probs([qi], h), v_ref[0, pl.ds(kd, tq), hp(h)])

        def pv_pair(i, carry):
            ks = pl.multiple_of(i * 2 * tq, 2 * tq)
            for h in heads:
                acc_sc[h] += _dot(probs([2 * i, 2 * i + 1], h), v_ref[0, pl.ds(ks, 2 * tq), hp(h)])
            return carry

        lax.fori_loop(0, n_pairs, pv_pair, 0)

        if t % 2 == 1:
            ks = pl.multiple_of((qi - 1) * tq, tq)
            for h in heads:
                acc_sc[h] += _dot(probs([qi - 1], h), v_ref[0, pl.ds(ks, tq), hp(h)])

        for h2 in range(MLA_HEADS // 2):
            outs = []
            for h in (2 * h2, 2 * h2 + 1):
                acc = acc_sc[h]
                outs.append(acc[:, :MLA_V] * (1.0 / acc[:, MLA_V:MLA_V + 1]))
            o_ref[0, rows, h2 * 2 * MLA_V:(h2 + 1) * 2 * MLA_V] = jnp.concatenate(outs, axis=1).astype(BF16)


def _attention(q, k, v):
    bsz, seq, qk_w = q.shape
    tq = min(TQ_ATTN, seq // 2)
    tiles = ATTN_TILES if seq % (ATTN_TILES * tq) == 0 else 2
    o_w = MLA_HEADS * MLA_V
    return pl.pallas_call(
        functools.partial(_attn_kernel, tq=tq, tiles=tiles),
        out_shape=jax.ShapeDtypeStruct((bsz, seq, o_w), BF16),
        grid=(bsz, seq // (tiles * tq)),
        in_specs=[pl.BlockSpec((1, tiles * tq, qk_w), lambda b, i: (b, i, 0)),
                  pl.BlockSpec((1, seq, qk_w), lambda b, i: (b, 0, 0)),
                  pl.BlockSpec((1, seq, qk_w), lambda b, i: (b, 0, 0))],
        out_specs=pl.BlockSpec((1, tiles * tq, o_w), lambda b, i: (b, i, 0)),
        scratch_shapes=[pltpu.VMEM((seq // tq, MLA_HEADS, tq, tq), F32),
                        pltpu.VMEM((MLA_HEADS, tq, tq // 2), F32),
                        pltpu.VMEM((MLA_HEADS, tq, MLA_HEAD_PAD), F32)],
        compiler_params=pltpu.CompilerParams(dimension_semantics=("arbitrary", "arbitrary"),
                                             vmem_limit_bytes=VMEM_LIMIT),
        name="mla_attention",
    )(q, k, v)


def _mix_back_kernel(x_ref, mp_ref, gm_ref, ym_ref, wbm_ref, wo_ref, nf_ref, wfi_ref, wfo_ref,
                     fn_ref, o_ref, *, final):
    merged = mp_ref[...].astype(F32) + gm_ref[...].astype(F32) * _dot(ym_ref[...], wbm_ref[...])
    x1 = x_ref[...] + _dot(merged.astype(BF16), wo_ref[...])
    hn = _rms(x1, nf_ref[...]).astype(BF16)
    acc = jnp.zeros_like(x1)
    for c in range(D_FF // FFN_COLS):
        g = _dot(hn, wfi_ref[:, c * FFN_COLS:(c + 1) * FFN_COLS])
        u = _dot(hn, wfi_ref[:, D_FF + c * FFN_COLS:D_FF + (c + 1) * FFN_COLS])
        act = (g * _sigmoid(g) * u).astype(BF16)
        acc = acc + _dot(act, wfo_ref[c * FFN_COLS:(c + 1) * FFN_COLS, :])
    x2 = x1 + acc
    if final:
        x2 = _rms(x2, fn_ref[...])
    o_ref[...] = x2


def _mix_back(layer, x, mp, gm, ym, w, norm_ffn, final_norm, final):
    t = x.shape[0]
    tm = min(TM_BACK, t)
    tile = lambda wd: pl.BlockSpec((tm, wd), lambda i: (i, 0))
    return pl.pallas_call(
        functools.partial(_mix_back_kernel, final=final),
        out_shape=jax.ShapeDtypeStruct((t, D_MODEL), F32),
        grid=(t // tm,),
        in_specs=[tile(D_MODEL), tile(D_MODEL), tile(D_MODEL), tile(BRANCH_W),
                  _layer_spec(w["wb"], layer, lead=(N_BRANCH - 1,)), _layer_spec(w["wo"], layer),
                  _layer_spec(norm_ffn, layer), _layer_spec(w["wfi"], layer), _layer_spec(w["wfo"], layer),
                  _const_spec(final_norm)],
        out_specs=tile(D_MODEL),
        compiler_params=pltpu.CompilerParams(dimension_semantics=("arbitrary",),
                                             vmem_limit_bytes=VMEM_LIMIT),
        name="mix_back_ffn",
    )(x, mp, gm, ym, w["wb"], w["wo"], norm_ffn, w["wfi"], w["wfo"], final_norm)


def kernel(x, positions, norm_mix, w_in, b_gate, conv_w, sg_ln_g, sg_ln_b, sg_ws, sg_b, mla_q_norm,
           mla_w_uq, mla_kv_norm, mla_w_ukv, w_branch, w_out, norm_ffn, w_ffn_in, w_ffn_out, final_norm):
    bsz, seq, d = x.shape
    depth = w_in.shape[0]
    tab = _rope_tables(positions)
    w = _prep_weights(w_in, mla_w_uq, mla_w_ukv, w_branch, w_out, w_ffn_in, w_ffn_out)
    ret_tabs = _retention_tables()
    row = lambda a: a.reshape(depth, 1, a.shape[-1])
    sgb = jnp.broadcast_to(sg_b[:, :, :, None], (depth, SG_GROUPS, CHUNK, CHUNK))
    norm_mix, sg_ln_g, sg_ln_b = row(norm_mix), row(sg_ln_g), row(sg_ln_b)
    mla_q_norm, mla_kv_norm, norm_ffn = row(mla_q_norm), row(mla_kv_norm), row(norm_ffn)
    final_norm = final_norm.reshape(1, d)
    flat = lambda a: a.reshape(bsz * seq, a.shape[-1])
    for l in range(depth):
        mp, gm, q, k, v = _mix_front(l, x, tab, w, norm_mix, b_gate, conv_w, sg_ln_g, sg_ln_b, sg_ws, sgb,
                                     mla_q_norm, mla_kv_norm, ret_tabs)
        ym = _attention(q, k, v)
        x = _mix_back(l, flat(x), flat(mp), flat(gm), flat(ym), w, norm_ffn, final_norm,
                      final=(l == depth - 1)).reshape(bsz, seq, d)
    return x
```
